```python
import jax, jax.numpy as jnp
from jax import lax
import numpy as np

D_MODEL = 1024
BATCH = 8
SEQ = 2048
DEPTH = 4
DEC_BATCH = 128
DEC_SEQ = 1
PAST_LEN = 16384
PAGE_SIZE = 128

D_A = D_MODEL
CHUNK = 128
GW_A = 128
G_A = D_A // GW_A
D_B = D_MODEL
H_B = 16
BW_B = D_B // H_B
CONV_W = 4
C_RG = 8.0
EPS = 1e-6
SPLIT_SIZES = (D_A, D_A, D_A, D_B, D_B, D_MODEL, D_MODEL)
SPLIT_IDX = tuple(int(s) for s in np.cumsum(SPLIT_SIZES)[:-1])
D_IN = int(sum(SPLIT_SIZES))

kernel_name = "hybrid_gmlp_rglru_decoder_step"


def rmsnorm(x, g):
    xf = x.astype(jnp.float32)
    r = xf * lax.rsqrt(jnp.mean(xf * xf, axis=-1, keepdims=True) + EPS)
    return (r * g.astype(jnp.float32)).astype(x.dtype)


def chunk_spatial_gate(u, v, w_s, b_s):
    B, T, _ = v.shape
    n_chunks = -(-T // CHUNK)
    pad = n_chunks * CHUNK - T
    vp = jnp.pad(v, ((0, 0), (0, pad), (0, 0))).reshape(B, n_chunks, CHUNK, G_A, GW_A)
    mask = jnp.tril(jnp.ones((CHUNK, CHUNK), dtype=bool))
    ws = jnp.where(mask[None], w_s, jnp.zeros_like(w_s))
    s = jnp.einsum('gts,bnsgc->bntgc', ws, vp) + b_s.T[None, None, :, :, None]
    s = s.reshape(B, n_chunks * CHUNK, D_A)[:, :T]
    return u * s


def causal_conv(x, buf, conv_w, conv_b):
    T = x.shape[1]
    xp = jnp.concatenate([buf.astype(x.dtype), x], axis=1)
    out = conv_b
    for k in range(CONV_W):
        out = out + conv_w[k] * xp[:, k:k + T]
    return out, xp[:, -(CONV_W - 1):]


def rg_lru(x, h0, w_a, b_a, w_x, b_x, lam):
    B, T, _ = x.shape
    xh = x.reshape(B, T, H_B, BW_B)
    r = jax.nn.sigmoid(jnp.einsum('bthi,hij->bthj', xh, w_a).reshape(B, T, D_B) + b_a)
    i = jax.nn.sigmoid(jnp.einsum('bthi,hij->bthj', xh, w_x).reshape(B, T, D_B) + b_x)
    log_a = -C_RG * r.astype(jnp.float32) * jax.nn.softplus(-lam.astype(jnp.float32))
    a = jnp.exp(log_a)
    xs = jnp.sqrt(-jnp.expm1(2.0 * log_a)) * (i * x).astype(jnp.float32)

    def step(h, inp):
        a_t, x_t = inp
        h = a_t * h + x_t
        return h, h

    h_last, hs = lax.scan(step, h0.astype(jnp.float32),
                          (jnp.swapaxes(a, 0, 1), jnp.swapaxes(xs, 0, 1)))
    return jnp.swapaxes(hs, 0, 1).astype(x.dtype), h_last


def mixer_layer(x, c, h0, conv_buf, w_ada, b_ada, norm_g, w_in, v_norm_g, w_s, b_s,
                conv_w, conv_b, w_rg_a, b_rg_a, w_rg_x, b_rg_x, lam, w_pa, w_pb, w_out):
    mod = jax.nn.silu(c) @ w_ada + b_ada
    shift, scale, gate = jnp.split(mod, 3, axis=-1)
    h = rmsnorm(x, norm_g) * (1.0 + scale[:, None]) + shift[:, None]
    z = h @ w_in
    u, v, g_a, x_b, g_b, z_a, z_b = jnp.split(z, SPLIT_IDX, axis=-1)
    v = rmsnorm(v, v_norm_g)
    y_a = chunk_spatial_gate(u, v, w_s, b_s) * jax.nn.silu(g_a)
    xc, conv_new = causal_conv(x_b, conv_buf, conv_w, conv_b)
    y_rnn, h_last = rg_lru(xc, h0, w_rg_a, b_rg_a, w_rg_x, b_rg_x, lam)
    y_b = y_rnn * jax.nn.silu(g_b)
    merged = jax.nn.sigmoid(z_a) * (y_a @ w_pa) + jax.nn.sigmoid(z_b) * (y_b @ w_pb)
    out = x + gate[:, None] * (merged @ w_out)
    return out, h_last, conv_new, v


def setup_inputs(seed: int = 0) -> dict:
    key = jax.random.key(seed)
    ks = jax.random.split(key, 24)
    f32 = jnp.float32
    n = lambda k, shape, s: jax.random.normal(k, shape, f32) * s
    a0 = jax.random.uniform(ks[19], (DEPTH, D_B), f32, 0.9, 0.999)
    p = a0 ** (1.0 / C_RG)
    lam = jnp.log(p) - jnp.log1p(-p)
    return {
        "x_prompt": n(ks[0], (BATCH, SEQ, D_MODEL), 1.0),
        "x_sample": n(ks[1], (DEC_BATCH, DEC_SEQ, D_MODEL), 1.0),
        "c_prompt": n(ks[2], (BATCH, D_MODEL), 1.0),
        "c_sample": n(ks[3], (DEC_BATCH, D_MODEL), 1.0),
        "state_rglru_h": n(ks[4], (DEPTH, DEC_BATCH, D_B), 0.5),
        "state_conv": n(ks[5], (DEPTH, DEC_BATCH, CONV_W - 1, D_B), 1.0),
        "w_ada": n(ks[6], (DEPTH, D_MODEL, 3 * D_MODEL), 0.5 * D_MODEL ** -0.5),
        "b_ada": n(ks[7], (DEPTH, 3 * D_MODEL), 0.01),
        "norm_g": 1.0 + n(ks[8], (DEPTH, D_MODEL), 0.05),
        "w_in": n(ks[9], (DEPTH, D_MODEL, D_IN), D_MODEL ** -0.5),
        "v_norm_g": 1.0 + n(ks[10], (DEPTH, D_A), 0.05),
        "w_s": n(ks[11], (DEPTH, G_A, CHUNK, CHUNK), CHUNK ** -0.5),
        "b_s": 1.0 + n(ks[12], (DEPTH, G_A, CHUNK), 0.1),
        "conv_w": n(ks[13], (DEPTH, CONV_W, D_B), CONV_W ** -0.5),
        "conv_b": n(ks[14], (DEPTH, D_B), 0.01),
        "w_rg_a": n(ks[15], (DEPTH, H_B, BW_B, BW_B), BW_B ** -0.5),
        "b_rg_a": n(ks[16], (DEPTH, D_B), 0.01),
        "w_rg_x": n(ks[17], (DEPTH, H_B, BW_B, BW_B), BW_B ** -0.5),
        "b_rg_x": n(ks[18], (DEPTH, D_B), 0.01),
        "lam": lam,
        "w_pa": n(ks[20], (DEPTH, D_A, D_MODEL), D_A ** -0.5),
        "w_pb": n(ks[21], (DEPTH, D_B, D_MODEL), D_B ** -0.5),
        "w_out": n(ks[22], (DEPTH, D_MODEL, D_MODEL), D_MODEL ** -0.5),
        "final_g": 1.0 + n(ks[23], (D_MODEL,), 0.05),
    }


def reference(x_prompt, x_sample, c_prompt, c_sample, state_rglru_h, state_conv,
              w_ada, b_ada, norm_g, w_in, v_norm_g, w_s, b_s, conv_w, conv_b,
              w_rg_a, b_rg_a, w_rg_x, b_rg_x, lam, w_pa, w_pb, w_out, final_g):
    xp, xs = x_prompt, x_sample
    hp_list, cp_list, hs_list, cs_list, vs_list = [], [], [], [], []
    for l in range(DEPTH):
        params = (w_ada[l], b_ada[l], norm_g[l], w_in[l], v_norm_g[l], w_s[l], b_s[l],
                  conv_w[l], conv_b[l], w_rg_a[l], b_rg_a[l], w_rg_x[l], b_rg_x[l],
                  lam[l], w_pa[l], w_pb[l], w_out[l])
        h0_p = jnp.zeros((xp.shape[0], D_B), jnp.float32)
        buf_p = jnp.zeros((xp.shape[0], CONV_W - 1, D_B), xp.dtype)
        xp, hp, cp, _ = mixer_layer(xp, c_prompt, h0_p, buf_p, *params)
        xs, hs, cs, vs = mixer_layer(xs, c_sample, state_rglru_h[l], state_conv[l], *params)
        hp_list.append(hp)
        cp_list.append(cp)
        hs_list.append(hs)
        cs_list.append(cs)
        vs_list.append(vs)
    y_prompt = rmsnorm(xp, final_g)
    y_sample = rmsnorm(xs, final_g)
    h_prompt = jnp.stack(hp_list)
    conv_prompt = jnp.stack(cp_list)
    h_sample = jnp.stack(hs_list)
    conv_sample = jnp.stack(cs_list)
    chunk_v_sample = jnp.stack(vs_list)
    return (y_prompt, y_sample, h_prompt, conv_prompt, h_sample, conv_sample, chunk_v_sample)
```

```python
import functools

import jax
import jax.numpy as jnp
from jax import lax
from jax.experimental import pallas as pl
from jax.experimental.pallas import tpu as pltpu

D = 1024
NB = 8
SEQ = 2048
DEPTH = 4
NS = 128
CHUNK = 128
GW = 128
NG = D // GW
HB = 16
BW = D // HB
CONV_W = 4
C_RG = 8.0
EPS = 1e-6
D_IN = 7 * D
C_U, C_V, C_GA, C_XB, C_GB, C_ZA, C_ZB = (i * D for i in range(7))

TT = 64
TM = TT * NB
NT = SEQ // TT
RB = 64
NRB = TM // RB
GBLK = 256
NGB = D // GBLK
TAIL = (CONV_W - 1) * NB

VMEM_LIMIT = 58 * 1024 * 1024

f32 = jnp.float32
bf16 = jnp.bfloat16


def _sigmoid(x):
    return jax.nn.sigmoid(x)


def _silu(x):
    return x * jax.nn.sigmoid(x)


def _neg_c_softplus(lam):
    y = -lam
    sp = jnp.maximum(y, 0.0) + jnp.log1p(jnp.exp(-jnp.abs(y)))
    return -C_RG * sp


def _rms_scale(x):
    return lax.rsqrt(jnp.mean(x * x, axis=-1, keepdims=True) + EPS)


def _mod_kernel(c_ref, w_ref, b_ref, o_ref):
    a = _silu(c_ref[...]).astype(bf16)
    w = w_ref[0].astype(bf16)
    o_ref[0] = jnp.dot(a, w, preferred_element_type=f32) + b_ref[0]


def _modulation(c_all, w_ada, b_ada):
    m = c_all.shape[0]
    return pl.pallas_call(
        _mod_kernel,
        grid=(DEPTH, 3),
        in_specs=[
            pl.BlockSpec((m, D), lambda l, j: (0, 0)),
            pl.BlockSpec((1, D, D), lambda l, j: (l, 0, j)),
            pl.BlockSpec((1, 1, D), lambda l, j: (l, 0, j)),
        ],
        out_specs=pl.BlockSpec((1, m, D), lambda l, j: (l, 0, j)),
        out_shape=jax.ShapeDtypeStruct((DEPTH, m, 3 * D), f32),
        compiler_params=pltpu.CompilerParams(
            dimension_semantics=("arbitrary", "arbitrary")),
        name="adaln_mod",
    )(c_all, w_ada, b_ada.reshape(DEPTH, 1, 3 * D))


def _prompt_layer_kernel(final, x_ref, mod_ref, ng_ref, win_ref, vng_ref, ws_ref,
                         bs_ref, cw_ref, cb_ref, wga_ref, wgx_ref, brg_ref, lam_ref,
                         wpa_ref, wpb_ref, wout_ref, fg_ref,
                         out_ref, hl_ref, cn_ref,
                         h_s, zb, slab, vprime, xb_s, hst, ya_s, yb_s, xcb_s):
    i = pl.program_id(0)
    half = i % 2
    hrow = pl.multiple_of(half * TT, TT)

    @pl.when(i == 0)
    def _():
        hst[...] = jnp.zeros_like(hst)
        xb_s[0:TAIL, :] = jnp.zeros((TAIL, D), f32)

    @pl.when(half == 0)
    def _():
        vprime[:, TT:CHUNK, :] = jnp.zeros((NG, CHUNK - TT, NB * GW), bf16)

    shift = mod_ref[0]
    scale1 = 1.0 + mod_ref[1]
    gate = mod_ref[2]
    ng = ng_ref[...]

    def p1(j, c):
        xv = x_ref[pl.ds(pl.multiple_of(j * 8, 8), 8)]
        hv = (xv * _rms_scale(xv) * ng) * scale1[None] + shift[None]
        h_s[pl.ds(pl.multiple_of(j * RB, RB), RB), :] = hv.reshape(RB, D).astype(bf16)
        return c
    lax.fori_loop(0, NRB, p1, 0)

    zb[:, 0:D] = jnp.dot(h_s[...], win_ref[:, C_V:C_V + D], preferred_element_type=f32)
    vng = vng_ref[...]

    def p2(j, c):
        rows = pl.ds(pl.multiple_of(j * RB, RB), RB)
        vv = zb[rows, 0:D]
        vn = vv * _rms_scale(vv) * vng
        for g in range(NG):
            slab[g, rows, :] = vn[:, g * GW:(g + 1) * GW]
        return c
    lax.fori_loop(0, NRB, p2, 0)

    for g in range(NG):
        for b in range(NB):
            vprime[g, pl.ds(hrow, TT), b * GW:(b + 1) * GW] = (
                slab[g, pl.ds(b, TT, stride=NB), :].astype(bf16))

    t_idx = hrow + lax.broadcasted_iota(jnp.int32, (TT, CHUNK), 0)
    s_idx = lax.broadcasted_iota(jnp.int32, (TT, CHUNK), 1)
    causal = s_idx <= t_idx
    for g in range(NG):
        wt = jnp.where(causal, ws_ref[g, pl.ds(hrow, TT), :], jnp.zeros((), bf16))
        sp = jnp.dot(wt, vprime[g], preferred_element_type=f32)
        bias = bs_ref[g, pl.ds(hrow, TT), :]
        for b in range(NB):
            slab[g, pl.ds(b, TT, stride=NB), :] = sp[:, b * GW:(b + 1) * GW] + bias

    zb[:, 0:D] = jnp.dot(h_s[...], win_ref[:, C_U:C_U + D], preferred_element_type=f32)
    zb[:, D:2 * D] = jnp.dot(h_s[...], win_ref[:, C_GA:C_GA + D],
                             preferred_element_type=f32)

    def p3(j, c):
        rows = pl.ds(pl.multiple_of(j * RB, RB), RB)
        for g in range(NG):
            cols = slice(g * GW, (g + 1) * GW)
            u = zb[rows, cols]
            ga = zb[rows, D + g * GW:D + (g + 1) * GW]
            ya_s[rows, cols] = (u * slab[g, rows, :] * _silu(ga)).astype(bf16)
        return c
    lax.fori_loop(0, NRB, p3, 0)

    xb_s[TAIL:TAIL + TM, :] = jnp.dot(h_s[...], win_ref[:, C_XB:C_XB + D],
                                      preferred_element_type=f32)
    zb[:, D:2 * D] = jnp.dot(h_s[...], win_ref[:, C_GB:C_GB + D],
                             preferred_element_type=f32)
    cb = cb_ref[...]
    cw = [cw_ref[k:k + 1, :] for k in range(CONV_W)]

    def p4a(j, c):
        r0 = pl.multiple_of(j * RB, RB)
        xc = cb
        for k in range(CONV_W):
            xc = xc + cw[k] * xb_s[pl.ds(pl.multiple_of(r0 + k * NB, NB), RB), :]
        zb[pl.ds(r0, RB), 0:D] = xc
        xcb_s[pl.ds(r0, RB), :] = xc.astype(bf16)
        return c
    lax.fori_loop(0, NRB, p4a, 0)

    tail = xb_s[TM:TM + TAIL, :]
    cn_ref[...] = tail
    xb_s[0:TAIL, :] = tail

    for q in range(NGB):
        cols = slice(q * GBLK, (q + 1) * GBLK)
        zb[:, 2 * D + q * GBLK:2 * D + (q + 1) * GBLK] = jnp.dot(
            xcb_s[:, cols], wga_ref[q], preferred_element_type=f32)
        zb[:, 3 * D + q * GBLK:3 * D + (q + 1) * GBLK] = jnp.dot(
            xcb_s[:, cols], wgx_ref[q], preferred_element_type=f32)

    ba = brg_ref[0:1, :]
    bx = brg_ref[1:2, :]
    cneg = _neg_c_softplus(lam_ref[...])

    def p4b(j, h):
        rows = pl.ds(pl.multiple_of(j * RB, RB), RB)
        xc = zb[rows, 0:D]
        r = _sigmoid(zb[rows, 2 * D:3 * D] + ba)
        ig = _sigmoid(zb[rows, 3 * D:4 * D] + bx)
        a = jnp.exp(r * cneg)
        xs = jnp.sqrt(1.0 - a * a) * (ig * xc)
        ys = []
        for t in range(RB // NB):
            h = a[t * NB:(t + 1) * NB] * h + xs[t * NB:(t + 1) * NB]
            ys.append(h)
        yr = jnp.concatenate(ys, axis=0)
        yb_s[rows, :] = (yr * _silu(zb[rows, D:2 * D])).astype(bf16)
        return h
    h_fin = lax.fori_loop(0, NRB, p4b, hst[...])
    hst[...] = h_fin
    hl_ref[...] = h_fin

    zb[:, 0:D] = jnp.dot(ya_s[...], wpa_ref[...], preferred_element_type=f32)
    zb[:, D:2 * D] = jnp.dot(yb_s[...], wpb_ref[...], preferred_element_type=f32)
    zb[:, 2 * D:3 * D] = jnp.dot(h_s[...], win_ref[:, C_ZA:C_ZA + D],
                                 preferred_element_type=f32)
    zb[:, 3 * D:4 * D] = jnp.dot(h_s[...], win_ref[:, C_ZB:C_ZB + D],
                                 preferred_element_type=f32)

    def p5(j, c):
        rows = pl.ds(pl.multiple_of(j * RB, RB), RB)
        mg = (_sigmoid(zb[rows, 2 * D:3 * D]) * zb[rows, 0:D]
              + _sigmoid(zb[rows, 3 * D:4 * D]) * zb[rows, D:2 * D])
        xcb_s[rows, :] = mg.astype(bf16)
        return c
    lax.fori_loop(0, NRB, p5, 0)

    zb[:, 0:D] = jnp.dot(xcb_s[...], wout_ref[...], preferred_element_type=f32)
    fg = fg_ref[...]

    def p6(j, c):
        ts = pl.ds(pl.multiple_of(j * 8, 8), 8)
        o = zb[pl.ds(pl.multiple_of(j * RB, RB), RB), 0:D].reshape(8, NB, D)
        y = x_ref[ts] + gate[None] * o
        if final:
            y = y * _rms_scale(y) * fg
        out_ref[ts] = y
        return c
    lax.fori_loop(0, NRB, p6, 0)


def _const_spec(shape):
    nd = len(shape)
    return pl.BlockSpec(shape, lambda i: (0,) * nd, pipeline_mode=pl.Buffered(1))


def _prompt_layer(final, x_tb, mod_l, ng, win, vng, ws, bs, cw, cb, wga, wgx, brg, lam,
                  wpa, wpb, wout, fg):
    consts = (mod_l, ng, win, vng, ws, bs, cw, cb, wga, wgx, brg, lam, wpa, wpb, wout, fg)
    return pl.pallas_call(
        functools.partial(_prompt_layer_kernel, final),
        grid=(NT,),
        in_specs=[pl.BlockSpec((TT, NB, D), lambda i: (i, 0, 0))]
        + [_const_spec(a.shape) for a in consts],
        out_specs=[
            pl.BlockSpec((TT, NB, D), lambda i: (i, 0, 0)),
            pl.BlockSpec((NB, D), lambda i: (0, 0)),
            pl.BlockSpec((TAIL, D), lambda i: (0, 0)),
        ],
        out_shape=[
            jax.ShapeDtypeStruct((SEQ, NB, D), f32),
            jax.ShapeDtypeStruct((NB, D), f32),
            jax.ShapeDtypeStruct((TAIL, D), f32),
        ],
        scratch_shapes=[
            pltpu.VMEM((TM, D), bf16),
            pltpu.VMEM((TM, 4 * D), f32),
            pltpu.VMEM((NG, TM, GW), f32),
            pltpu.VMEM((NG, CHUNK, NB * GW), bf16),
            pltpu.VMEM((TM + TAIL, D), f32),
            pltpu.VMEM((NB, D), f32),
            pltpu.VMEM((TM, D), bf16),
            pltpu.VMEM((TM, D), bf16),
            pltpu.VMEM((TM, D), bf16),
        ],
        compiler_params=pltpu.CompilerParams(
            dimension_semantics=("arbitrary",), vmem_limit_bytes=VMEM_LIMIT),
        name="prompt_layer_final" if final else "prompt_layer",
    )(x_tb, *consts)


def _sample_kernel(x_ref, mod_ref, h0_ref, cbuf_ref, ng_ref, win_ref, vng_ref, ws0_ref,
                   bs0_ref, cw_ref, cb_ref, wga_ref, wgx_ref, brg_ref, lam_ref,
                   wpa_ref, wpb_ref, wout_ref, fg_ref,
                   y_ref, hs_ref, cs_ref, vs_ref, xs_s):
    l = pl.program_id(0)

    @pl.when(l == 0)
    def _():
        xs_s[...] = x_ref[...]

    x = xs_s[...]
    shift = mod_ref[0, :, 0:D]
    scale = mod_ref[0, :, D:2 * D]
    gate = mod_ref[0, :, 2 * D:3 * D]
    h = ((x * _rms_scale(x) * ng_ref[0]) * (1.0 + scale) + shift).astype(bf16)

    def proj(c0):
        return jnp.dot(h, win_ref[0, :, c0:c0 + D], preferred_element_type=f32)

    v = proj(C_V)
    v = v * _rms_scale(v) * vng_ref[0]
    vs_ref[0] = v
    s = ws0_ref[0] * v + bs0_ref[0]
    ya = (proj(C_U) * s * _silu(proj(C_GA))).astype(bf16)

    xb = proj(C_XB)
    xc = (cb_ref[0] + cw_ref[0, 0:1, :] * cbuf_ref[0, 0] + cw_ref[0, 1:2, :] * cbuf_ref[0, 1]
          + cw_ref[0, 2:3, :] * cbuf_ref[0, 2] + cw_ref[0, 3:4, :] * xb)
    cs_ref[0, 0] = cbuf_ref[0, 1]
    cs_ref[0, 1] = cbuf_ref[0, 2]
    cs_ref[0, 2] = xb
    xcb = xc.astype(bf16)
    rp = jnp.concatenate(
        [jnp.dot(xcb[:, q * GBLK:(q + 1) * GBLK], wga_ref[0, q], preferred_element_type=f32)
         for q in range(NGB)], axis=1)
    ip = jnp.concatenate(
        [jnp.dot(xcb[:, q * GBLK:(q + 1) * GBLK], wgx_ref[0, q], preferred_element_type=f32)
         for q in range(NGB)], axis=1)
    r = _sigmoid(rp + brg_ref[0, 0:1, :])
    ig = _sigmoid(ip + brg_ref[0, 1:2, :])
    a = jnp.exp(r * _neg_c_softplus(lam_ref[0]))
    hn = a * h0_ref[0] + jnp.sqrt(1.0 - a * a) * (ig * xc)
    hs_ref[0] = hn
    yb = (hn * _silu(proj(C_GB))).astype(bf16)

    pa = jnp.dot(ya, wpa_ref[0], preferred_element_type=f32)
    pb = jnp.dot(yb, wpb_ref[0], preferred_element_type=f32)
    mg = (_sigmoid(proj(C_ZA)) * pa + _sigmoid(proj(C_ZB)) * pb).astype(bf16)
    xn = x + gate * jnp.dot(mg, wout_ref[0], preferred_element_type=f32)
    xs_s[...] = xn

    @pl.when(l == DEPTH - 1)
    def _():
        y_ref[...] = xn * _rms_scale(xn) * fg_ref[...]


def _layer_spec(shape):
    nd = len(shape)
    return pl.BlockSpec((1,) + tuple(shape[1:]), lambda l: (l,) + (0,) * (nd - 1))


def _sample_group(x_s, mod_s, h0, cbuf, ng, win, vng, ws0, bs0, cw, cb, wga, wgx, brg, lam,
                  wpa, wpb, wout, fg):
    per_layer = (mod_s, h0, cbuf, ng, win, vng, ws0, bs0, cw, cb, wga, wgx, brg, lam,
                 wpa, wpb, wout)
    return pl.pallas_call(
        _sample_kernel,
        grid=(DEPTH,),
        in_specs=[pl.BlockSpec((NS, D), lambda l: (0, 0))]
        + [_layer_spec(a.shape) for a in per_layer]
        + [pl.BlockSpec((1, D), lambda l: (0, 0))],
        out_specs=[
            pl.BlockSpec((NS, D), lambda l: (0, 0)),
            pl.BlockSpec((1, NS, D), lambda l: (l, 0, 0)),
            pl.BlockSpec((1, CONV_W - 1, NS, D), lambda l: (l, 0, 0, 0)),
            pl.BlockSpec((1, NS, D), lambda l: (l, 0, 0)),
        ],
        out_shape=[
            jax.ShapeDtypeStruct((NS, D), f32),
            jax.ShapeDtypeStruct((DEPTH, NS, D), f32),
            jax.ShapeDtypeStruct((DEPTH, CONV_W - 1, NS, D), f32),
            jax.ShapeDtypeStruct((DEPTH, NS, D), f32),
        ],
        scratch_shapes=[pltpu.VMEM((NS, D), f32)],
        compiler_params=pltpu.CompilerParams(
            dimension_semantics=("arbitrary",), vmem_limit_bytes=VMEM_LIMIT),
        name="sample_layers",
    )(x_s, *per_layer, fg)


def _block_diag_gate(w):
    per = GBLK // BW
    w = w.reshape(DEPTH, NGB, per, BW, BW)
    eye = jnp.eye(per, dtype=w.dtype)
    out = jnp.einsum('lqhij,hk->lqhikj', w, eye)
    return out.reshape(DEPTH, NGB, GBLK, GBLK)


def kernel(x_prompt, x_sample, c_prompt, c_sample, state_rglru_h, state_conv, w_ada, b_ada,
           norm_g, w_in, v_norm_g, w_s, b_s, conv_w, conv_b, w_rg_a, b_rg_a, w_rg_x,
           b_rg_x, lam, w_pa, w_pb, w_out, final_g):
    c_all = jnp.concatenate([c_sample, c_prompt, jnp.zeros((8, D), f32)], axis=0)
    mod = _modulation(c_all, w_ada, b_ada)
    mod_s = mod[:, :NS]
    mod_p = mod[:, NS:NS + NB].reshape(DEPTH, NB, 3, D).transpose(0, 2, 1, 3)

    win = w_in.astype(bf16)
    wpa = w_pa.astype(bf16)
    wpb = w_pb.astype(bf16)
    wout = w_out.astype(bf16)
    wga = _block_diag_gate(w_rg_a).astype(bf16)
    wgx = _block_diag_gate(w_rg_x).astype(bf16)
    ws = w_s.astype(bf16)
    bs = jnp.broadcast_to(b_s[..., None], (DEPTH, NG, CHUNK, GW))
    ng = norm_g.reshape(DEPTH, 1, D)
    vng = v_norm_g.reshape(DEPTH, 1, D)
    cb = conv_b.reshape(DEPTH, 1, D)
    brg = jnp.stack([b_rg_a, b_rg_x], axis=1)
    lam3 = lam.reshape(DEPTH, 1, D)
    fg = final_g.reshape(1, D)

    x_tb = x_prompt.transpose(1, 0, 2)
    hp, cp = [], []
    for l in range(DEPTH):
        x_tb, h_l, c_l = _prompt_layer(
            l == DEPTH - 1, x_tb, mod_p[l], ng[l], win[l], vng[l], ws[l], bs[l],
            conv_w[l], cb[l], wga[l], wgx[l], brg[l], lam3[l], wpa[l], wpb[l], wout[l], fg)
        hp.append(h_l)
        cp.append(c_l.reshape(CONV_W - 1, NB, D).transpose(1, 0, 2))
    y_prompt = x_tb.transpose(1, 0, 2)
    h_prompt = jnp.stack(hp)
    conv_prompt = jnp.stack(cp)

    ws0 = jnp.repeat(w_s[:, :, 0, 0], GW, axis=-1).reshape(DEPTH, 1, D)
    bs0 = jnp.repeat(b_s[:, :, 0], GW, axis=-1).reshape(DEPTH, 1, D)
    cbuf = state_conv.transpose(0, 2, 1, 3)
    y_s, h_sample, cs, vs = _sample_group(
        x_sample.reshape(NS, D), mod_s, state_rglru_h, cbuf, ng, win, vng, ws0, bs0,
        conv_w, cb, wga, wgx, brg, lam3, wpa, wpb, wout, fg)
    y_sample = y_s.reshape(NS, 1, D)
    conv_sample = cs.transpose(0, 2, 1, 3)
    chunk_v_sample = vs.reshape(DEPTH, NS, 1, D)
    return (y_prompt, y_sample, h_prompt, conv_prompt, h_sample, conv_sample,
            chunk_v_sample)
```

```python
import functools

import jax
import jax.numpy as jnp
from jax import lax
from jax.experimental import pallas as pl
from jax.experimental.pallas import tpu as pltpu

D = 1024
NB = 8
SEQ = 2048
DEPTH = 4
NS = 128
CHUNK = 128
GW = 128
NG = D // GW
HB = 16
BW = D // HB
CONV_W = 4
C_RG = 8.0
EPS = 1e-6
D_IN = 7 * D
C_U, C_V, C_GA, C_XB, C_GB, C_ZA, C_ZB = (i * D for i in range(7))

TT = 64
TM = TT * NB
NT = SEQ // TT
RB = 64
NRB = TM // RB
GBLK = 256
NGB = D // GBLK
TAIL = (CONV_W - 1) * NB

VMEM_LIMIT = 58 * 1024 * 1024

f32 = jnp.float32
bf16 = jnp.bfloat16


def _sigmoid(x):
    return jax.nn.sigmoid(x)


def _silu(x):
    return x * jax.nn.sigmoid(x)


def _neg_c_softplus(lam):
    y = -lam
    sp = jnp.maximum(y, 0.0) + jnp.log1p(jnp.exp(-jnp.abs(y)))
    return -C_RG * sp


def _rms_scale(x):
    return lax.rsqrt(jnp.mean(x * x, axis=-1, keepdims=True) + EPS)


def _mod_kernel(c_ref, w_ref, b_ref, o_ref):
    a = _silu(c_ref[...]).astype(bf16)
    w = w_ref[0].astype(bf16)
    o_ref[0] = jnp.dot(a, w, preferred_element_type=f32) + b_ref[0]


def _modulation(c_all, w_ada, b_ada):
    m = c_all.shape[0]
    return pl.pallas_call(
        _mod_kernel,
        grid=(DEPTH, 3),
        in_specs=[
            pl.BlockSpec((m, D), lambda l, j: (0, 0)),
            pl.BlockSpec((1, D, D), lambda l, j: (l, 0, j)),
            pl.BlockSpec((1, 1, D), lambda l, j: (l, 0, j)),
        ],
        out_specs=pl.BlockSpec((1, m, D), lambda l, j: (l, 0, j)),
        out_shape=jax.ShapeDtypeStruct((DEPTH, m, 3 * D), f32),
        compiler_params=pltpu.CompilerParams(
            dimension_semantics=("arbitrary", "arbitrary")),
        name="adaln_mod",
    )(c_all, w_ada, b_ada.reshape(DEPTH, 1, 3 * D))


def _prompt_layer_kernel(final, x_ref, mod_ref, ng_ref, win_ref, vng_ref, ws_ref,
                         bs_ref, cw_ref, cb_ref, wga_ref, wgx_ref, brg_ref, lam_ref,
                         wpa_ref, wpb_ref, wout_ref, fg_ref,
                         out_ref, hl_ref, cn_ref,
                         h_s, zb, slab, vprime, xb_s, hst, ya_s, yb_s, xcb_s):
    i = pl.program_id(0)
    half = i % 2
    hrow = pl.multiple_of(half * TT, TT)

    @pl.when(i == 0)
    def _():
        hst[...] = jnp.zeros_like(hst)
        xb_s[0:TAIL, :] = jnp.zeros((TAIL, D), f32)

    @pl.when(half == 0)
    def _():
        vprime[:, TT:CHUNK, :] = jnp.zeros((NG, CHUNK - TT, NB * GW), bf16)

    shift = mod_ref[0]
    scale1 = 1.0 + mod_ref[1]
    gate = mod_ref[2]
    ng = ng_ref[...]

    def p1(j, c):
        xv = x_ref[pl.ds(pl.multiple_of(j * 8, 8), 8)]
        hv = (xv * _rms_scale(xv) * ng) * scale1[None] + shift[None]
        h_s[pl.ds(pl.multiple_of(j * RB, RB), RB), :] = hv.reshape(RB, D).astype(bf16)
        return c
    lax.fori_loop(0, NRB, p1, 0, unroll=True)

    zb[:, 0:D] = jnp.dot(h_s[...], win_ref[:, C_V:C_V + D], preferred_element_type=f32)
    vng = vng_ref[...]

    def p2(j, c):
        rows = pl.ds(pl.multiple_of(j * RB, RB), RB)
        vv = zb[rows, 0:D]
        vn = vv * _rms_scale(vv) * vng
        for g in range(NG):
            slab[g, rows, :] = vn[:, g * GW:(g + 1) * GW]
        return c
    lax.fori_loop(0, NRB, p2, 0, unroll=True)

    for g in range(NG):
        for b in range(NB):
            vprime[g, pl.ds(hrow, TT), b * GW:(b + 1) * GW] = (
                slab[g, pl.ds(b, TT, stride=NB), :].astype(bf16))

    t_idx = hrow + lax.broadcasted_iota(jnp.int32, (TT, CHUNK), 0)
    s_idx = lax.broadcasted_iota(jnp.int32, (TT, CHUNK), 1)
    causal = s_idx <= t_idx
    for g in range(NG):
        wt = jnp.where(causal, ws_ref[g, pl.ds(hrow, TT), :], jnp.zeros((), bf16))
        sp = jnp.dot(wt, vprime[g], preferred_element_type=f32)
        bias = bs_ref[g, pl.ds(hrow, TT), :]
        for b in range(NB):
            slab[g, pl.ds(b, TT, stride=NB), :] = sp[:, b * GW:(b + 1) * GW] + bias

    zb[:, 0:D] = jnp.dot(h_s[...], win_ref[:, C_U:C_U + D], preferred_element_type=f32)
    zb[:, D:2 * D] = jnp.dot(h_s[...], win_ref[:, C_GA:C_GA + D],
                             preferred_element_type=f32)

    def p3(j, c):
        rows = pl.ds(pl.multiple_of(j * RB, RB), RB)
        for g in range(NG):
            cols = slice(g * GW, (g + 1) * GW)
            u = zb[rows, cols]
            ga = zb[rows, D + g * GW:D + (g + 1) * GW]
            ya_s[rows, cols] = (u * slab[g, rows, :] * _silu(ga)).astype(bf16)
        return c
    lax.fori_loop(0, NRB, p3, 0, unroll=True)

    xb_s[TAIL:TAIL + TM, :] = jnp.dot(h_s[...], win_ref[:, C_XB:C_XB + D],
                                      preferred_element_type=f32)
    zb[:, D:2 * D] = jnp.dot(h_s[...], win_ref[:, C_GB:C_GB + D],
                             preferred_element_type=f32)
    cb = cb_ref[...]
    cw = [cw_ref[k:k + 1, :] for k in range(CONV_W)]

    def p4a(j, c):
        r0 = pl.multiple_of(j * RB, RB)
        xc = cb
        for k in range(CONV_W):
            xc = xc + cw[k] * xb_s[pl.ds(pl.multiple_of(r0 + k * NB, NB), RB), :]
        zb[pl.ds(r0, RB), 0:D] = xc
        xcb_s[pl.ds(r0, RB), :] = xc.astype(bf16)
        return c
    lax.fori_loop(0, NRB, p4a, 0, unroll=True)

    tail = xb_s[TM:TM + TAIL, :]
    cn_ref[...] = tail
    xb_s[0:TAIL, :] = tail

    for q in range(NGB):
        cols = slice(q * GBLK, (q + 1) * GBLK)
        zb[:, 2 * D + q * GBLK:2 * D + (q + 1) * GBLK] = jnp.dot(
            xcb_s[:, cols], wga_ref[q], preferred_element_type=f32)
        zb[:, 3 * D + q * GBLK:3 * D + (q + 1) * GBLK] = jnp.dot(
            xcb_s[:, cols], wgx_ref[q], preferred_element_type=f32)

    ba = brg_ref[0:1, :]
    bx = brg_ref[1:2, :]
    cneg = _neg_c_softplus(lam_ref[...])

    def p4b(j, h):
        rows = pl.ds(pl.multiple_of(j * RB, RB), RB)
        xc = zb[rows, 0:D]
        r = _sigmoid(zb[rows, 2 * D:3 * D] + ba)
        ig = _sigmoid(zb[rows, 3 * D:4 * D] + bx)
        a = jnp.exp(r * cneg)
        xs = jnp.sqrt(1.0 - a * a) * (ig * xc)
        ys = []
        for t in range(RB // NB):
            h = a[t * NB:(t + 1) * NB] * h + xs[t * NB:(t + 1) * NB]
            ys.append(h)
        yr = jnp.concatenate(ys, axis=0)
        yb_s[rows, :] = (yr * _silu(zb[rows, D:2 * D])).astype(bf16)
        return h
    h_fin = lax.fori_loop(0, NRB, p4b, hst[...], unroll=True)
    hst[...] = h_fin
    hl_ref[...] = h_fin

    zb[:, 0:D] = jnp.dot(ya_s[...], wpa_ref[...], preferred_element_type=f32)
    zb[:, D:2 * D] = jnp.dot(yb_s[...], wpb_ref[...], preferred_element_type=f32)
    zb[:, 2 * D:3 * D] = jnp.dot(h_s[...], win_ref[:, C_ZA:C_ZA + D],
                                 preferred_element_type=f32)
    zb[:, 3 * D:4 * D] = jnp.dot(h_s[...], win_ref[:, C_ZB:C_ZB + D],
                                 preferred_element_type=f32)

    def p5(j, c):
        rows = pl.ds(pl.multiple_of(j * RB, RB), RB)
        mg = (_sigmoid(zb[rows, 2 * D:3 * D]) * zb[rows, 0:D]
              + _sigmoid(zb[rows, 3 * D:4 * D]) * zb[rows, D:2 * D])
        xcb_s[rows, :] = mg.astype(bf16)
        return c
    lax.fori_loop(0, NRB, p5, 0, unroll=True)

    zb[:, 0:D] = jnp.dot(xcb_s[...], wout_ref[...], preferred_element_type=f32)
    fg = fg_ref[...]

    def p6(j, c):
        ts = pl.ds(pl.multiple_of(j * 8, 8), 8)
        o = zb[pl.ds(pl.multiple_of(j * RB, RB), RB), 0:D].reshape(8, NB, D)
        y = x_ref[ts] + gate[None] * o
        if final:
            y = y * _rms_scale(y) * fg
        out_ref[ts] = y
        return c
    lax.fori_loop(0, NRB, p6, 0, unroll=True)


def _const_spec(shape):
    nd = len(shape)
    return pl.BlockSpec(shape, lambda i: (0,) * nd, pipeline_mode=pl.Buffered(1))


def _prompt_layer(final, x_tb, mod_l, ng, win, vng, ws, bs, cw, cb, wga, wgx, brg, lam,
                  wpa, wpb, wout, fg):
    consts = (mod_l, ng, win, vng, ws, bs, cw, cb, wga, wgx, brg, lam, wpa, wpb, wout, fg)
    return pl.pallas_call(
        functools.partial(_prompt_layer_kernel, final),
        grid=(NT,),
        in_specs=[pl.BlockSpec((TT, NB, D), lambda i: (i, 0, 0))]
        + [_const_spec(a.shape) for a in consts],
        out_specs=[
            pl.BlockSpec((TT, NB, D), lambda i: (i, 0, 0)),
            pl.BlockSpec((NB, D), lambda i: (0, 0)),
            pl.BlockSpec((TAIL, D), lambda i: (0, 0)),
        ],
        out_shape=[
            jax.ShapeDtypeStruct((SEQ, NB, D), f32),
            jax.ShapeDtypeStruct((NB, D), f32),
            jax.ShapeDtypeStruct((TAIL, D), f32),
        ],
        scratch_shapes=[
            pltpu.VMEM((TM, D), bf16),
            pltpu.VMEM((TM, 4 * D), f32),
            pltpu.VMEM((NG, TM, GW), f32),
            pltpu.VMEM((NG, CHUNK, NB * GW), bf16),
            pltpu.VMEM((TM + TAIL, D), f32),
            pltpu.VMEM((NB, D), f32),
            pltpu.VMEM((TM, D), bf16),
            pltpu.VMEM((TM, D), bf16),
            pltpu.VMEM((TM, D), bf16),
        ],
        compiler_params=pltpu.CompilerParams(
            dimension_semantics=("arbitrary",), vmem_limit_bytes=VMEM_LIMIT),
        name="prompt_layer_final" if final else "prompt_layer",
    )(x_tb, *consts)


def _sample_kernel(x_ref, mod_ref, h0_ref, cbuf_ref, ng_ref, win_ref, vng_ref, ws0_ref,
                   bs0_ref, cw_ref, cb_ref, wga_ref, wgx_ref, brg_ref, lam_ref,
                   wpa_ref, wpb_ref, wout_ref, fg_ref,
                   y_ref, hs_ref, cs_ref, vs_ref, xs_s):
    l = pl.program_id(0)

    @pl.when(l == 0)
    def _():
        xs_s[...] = x_ref[...]

    x = xs_s[...]
    shift = mod_ref[0, :, 0:D]
    scale = mod_ref[0, :, D:2 * D]
    gate = mod_ref[0, :, 2 * D:3 * D]
    h = ((x * _rms_scale(x) * ng_ref[0]) * (1.0 + scale) + shift).astype(bf16)

    def proj(c0):
        return jnp.dot(h, win_ref[0, :, c0:c0 + D], preferred_element_type=f32)

    v = proj(C_V)
    v = v * _rms_scale(v) * vng_ref[0]
    vs_ref[0] = v
    s = ws0_ref[0] * v + bs0_ref[0]
    ya = (proj(C_U) * s * _silu(proj(C_GA))).astype(bf16)

    xb = proj(C_XB)
    xc = (cb_ref[0] + cw_ref[0, 0:1, :] * cbuf_ref[0, 0] + cw_ref[0, 1:2, :] * cbuf_ref[0, 1]
          + cw_ref[0, 2:3, :] * cbuf_ref[0, 2] + cw_ref[0, 3:4, :] * xb)
    cs_ref[0, 0] = cbuf_ref[0, 1]
    cs_ref[0, 1] = cbuf_ref[0, 2]
    cs_ref[0, 2] = xb
    xcb = xc.astype(bf16)
    rp = jnp.concatenate(
        [jnp.dot(xcb[:, q * GBLK:(q + 1) * GBLK], wga_ref[0, q], preferred_element_type=f32)
         for q in range(NGB)], axis=1)
    ip = jnp.concatenate(
        [jnp.dot(xcb[:, q * GBLK:(q + 1) * GBLK], wgx_ref[0, q], preferred_element_type=f32)
         for q in range(NGB)], axis=1)
    r = _sigmoid(rp + brg_ref[0, 0:1, :])
    ig = _sigmoid(ip + brg_ref[0, 1:2, :])
    a = jnp.exp(r * _neg_c_softplus(lam_ref[0]))
    hn = a * h0_ref[0] + jnp.sqrt(1.0 - a * a) * (ig * xc)
    hs_ref[0] = hn
    yb = (hn * _silu(proj(C_GB))).astype(bf16)

    pa = jnp.dot(ya, wpa_ref[0], preferred_element_type=f32)
    pb = jnp.dot(yb, wpb_ref[0], preferred_element_type=f32)
    mg = (_sigmoid(proj(C_ZA)) * pa + _sigmoid(proj(C_ZB)) * pb).astype(bf16)
    xn = x + gate * jnp.dot(mg, wout_ref[0], preferred_element_type=f32)
    xs_s[...] = xn

    @pl.when(l == DEPTH - 1)
    def _():
        y_ref[...] = xn * _rms_scale(xn) * fg_ref[...]


def _layer_spec(shape):
    nd = len(shape)
    return pl.BlockSpec((1,) + tuple(shape[1:]), lambda l: (l,) + (0,) * (nd - 1))


def _sample_group(x_s, mod_s, h0, cbuf, ng, win, vng, ws0, bs0, cw, cb, wga, wgx, brg, lam,
                  wpa, wpb, wout, fg):
    per_layer = (mod_s, h0, cbuf, ng, win, vng, ws0, bs0, cw, cb, wga, wgx, brg, lam,
                 wpa, wpb, wout)
    return pl.pallas_call(
        _sample_kernel,
        grid=(DEPTH,),
        in_specs=[pl.BlockSpec((NS, D), lambda l: (0, 0))]
        + [_layer_spec(a.shape) for a in per_layer]
        + [pl.BlockSpec((1, D), lambda l: (0, 0))],
        out_specs=[
            pl.BlockSpec((NS, D), lambda l: (0, 0)),
            pl.BlockSpec((1, NS, D), lambda l: (l, 0, 0)),
            pl.BlockSpec((1, CONV_W - 1, NS, D), lambda l: (l, 0, 0, 0)),
            pl.BlockSpec((1, NS, D), lambda l: (l, 0, 0)),
        ],
        out_shape=[
            jax.ShapeDtypeStruct((NS, D), f32),
            jax.ShapeDtypeStruct((DEPTH, NS, D), f32),
            jax.ShapeDtypeStruct((DEPTH, CONV_W - 1, NS, D), f32),
            jax.ShapeDtypeStruct((DEPTH, NS, D), f32),
        ],
        scratch_shapes=[pltpu.VMEM((NS, D), f32)],
        compiler_params=pltpu.CompilerParams(
            dimension_semantics=("arbitrary",), vmem_limit_bytes=VMEM_LIMIT),
        name="sample_layers",
    )(x_s, *per_layer, fg)


def _block_diag_gate(w):
    per = GBLK // BW
    w = w.reshape(DEPTH, NGB, per, BW, BW)
    eye = jnp.eye(per, dtype=w.dtype)
    out = jnp.einsum('lqhij,hk->lqhikj', w, eye)
    return out.reshape(DEPTH, NGB, GBLK, GBLK)


def kernel(x_prompt, x_sample, c_prompt, c_sample, state_rglru_h, state_conv, w_ada, b_ada,
           norm_g, w_in, v_norm_g, w_s, b_s, conv_w, conv_b, w_rg_a, b_rg_a, w_rg_x,
           b_rg_x, lam, w_pa, w_pb, w_out, final_g):
    c_all = jnp.concatenate([c_sample, c_prompt, jnp.zeros((8, D), f32)], axis=0)
    mod = _modulation(c_all, w_ada, b_ada)
    mod_s = mod[:, :NS]
    mod_p = mod[:, NS:NS + NB].reshape(DEPTH, NB, 3, D).transpose(0, 2, 1, 3)

    win = w_in.astype(bf16)
    wpa = w_pa.astype(bf16)
    wpb = w_pb.astype(bf16)
    wout = w_out.astype(bf16)
    wga = _block_diag_gate(w_rg_a).astype(bf16)
    wgx = _block_diag_gate(w_rg_x).astype(bf16)
    ws = w_s.astype(bf16)
    bs = jnp.broadcast_to(b_s[..., None], (DEPTH, NG, CHUNK, GW))
    ng = norm_g.reshape(DEPTH, 1, D)
    vng = v_norm_g.reshape(DEPTH, 1, D)
    cb = conv_b.reshape(DEPTH, 1, D)
    brg = jnp.stack([b_rg_a, b_rg_x], axis=1)
    lam3 = lam.reshape(DEPTH, 1, D)
    fg = final_g.reshape(1, D)

    x_tb = x_prompt.transpose(1, 0, 2)
    hp, cp = [], []
    for l in range(DEPTH):
        x_tb, h_l, c_l = _prompt_layer(
            l == DEPTH - 1, x_tb, mod_p[l], ng[l], win[l], vng[l], ws[l], bs[l],
            conv_w[l], cb[l], wga[l], wgx[l], brg[l], lam3[l], wpa[l], wpb[l], wout[l], fg)
        hp.append(h_l)
        cp.append(c_l.reshape(CONV_W - 1, NB, D).transpose(1, 0, 2))
    y_prompt = x_tb.transpose(1, 0, 2)
    h_prompt = jnp.stack(hp)
    conv_prompt = jnp.stack(cp)

    ws0 = jnp.repeat(w_s[:, :, 0, 0], GW, axis=-1).reshape(DEPTH, 1, D)
    bs0 = jnp.repeat(b_s[:, :, 0], GW, axis=-1).reshape(DEPTH, 1, D)
    cbuf = state_conv.transpose(0, 2, 1, 3)
    y_s, h_sample, cs, vs = _sample_group(
        x_sample.reshape(NS, D), mod_s, state_rglru_h, cbuf, ng, win, vng, ws0, bs0,
        conv_w, cb, wga, wgx, brg, lam3, wpa, wpb, wout, fg)
    y_sample = y_s.reshape(NS, 1, D)
    conv_sample = cs.transpose(0, 2, 1, 3)
    chunk_v_sample = vs.reshape(DEPTH, NS, 1, D)
    return (y_prompt, y_sample, h_prompt, conv_prompt, h_sample, conv_sample,
            chunk_v_sample)
```

```python
import functools

import jax
import jax.numpy as jnp
from jax import lax
from jax.experimental import pallas as pl
from jax.experimental.pallas import tpu as pltpu

D = 1024
NB = 8
SEQ = 2048
DEPTH = 4
NS = 128
CHUNK = 128
GW = 128
NG = D // GW
HB = 16
BW = D // HB
CONV_W = 4
C_RG = 8.0
EPS = 1e-6
D_IN = 7 * D
C_U, C_V, C_GA, C_XB, C_GB, C_ZA, C_ZB = (i * D for i in range(7))

TT = 64
TM = TT * NB
NT = SEQ // TT
RB = 64
NRB = TM // RB
GBLK = 256
NGB = D // GBLK
TAIL = (CONV_W - 1) * NB

VMEM_LIMIT = 62 * 1024 * 1024

f32 = jnp.float32
bf16 = jnp.bfloat16


def _sigmoid(x):
    return jax.nn.sigmoid(x)


def _silu(x):
    return x * jax.nn.sigmoid(x)


def _neg_c_softplus(lam):
    y = -lam
    sp = jnp.maximum(y, 0.0) + jnp.log1p(jnp.exp(-jnp.abs(y)))
    return -C_RG * sp


def _rms_scale(x):
    return lax.rsqrt(jnp.mean(x * x, axis=-1, keepdims=True) + EPS)


def _mod_kernel(c_ref, w_ref, b_ref, o_ref):
    a = _silu(c_ref[...]).astype(bf16)
    w = w_ref[0].astype(bf16)
    o_ref[0] = jnp.dot(a, w, preferred_element_type=f32) + b_ref[0]


def _modulation(c_all, w_ada, b_ada):
    m = c_all.shape[0]
    return pl.pallas_call(
        _mod_kernel,
        grid=(DEPTH, 3),
        in_specs=[
            pl.BlockSpec((m, D), lambda l, j: (0, 0)),
            pl.BlockSpec((1, D, D), lambda l, j: (l, 0, j)),
            pl.BlockSpec((1, 1, D), lambda l, j: (l, 0, j)),
        ],
        out_specs=pl.BlockSpec((1, m, D), lambda l, j: (l, 0, j)),
        out_shape=jax.ShapeDtypeStruct((DEPTH, m, 3 * D), f32),
        compiler_params=pltpu.CompilerParams(
            dimension_semantics=("arbitrary", "arbitrary")),
        name="adaln_mod",
    )(c_all, w_ada, b_ada.reshape(DEPTH, 1, 3 * D))


def _prompt_layer_kernel(final, x_ref, mod_ref, ng_ref, win_ref, vng_ref, ws_ref,
                         bs_ref, cw_ref, cb_ref, wga_ref, wgx_ref, brg_ref, lam_ref,
                         wpa_ref, wpb_ref, wout_ref, fg_ref,
                         out_ref, hl_ref, cn_ref,
                         h_s, bv, bu, bga, bgb, bxc, brp, bip, bpa, slab, vprime, xb_s,
                         hst, ya_s, yb_s, xcb_s):
    i = pl.program_id(0)
    half = i % 2
    hrow = pl.multiple_of(half * TT, TT)

    @pl.when(i == 0)
    def _():
        hst[...] = jnp.zeros_like(hst)
        xb_s[0:TAIL, :] = jnp.zeros((TAIL, D), f32)

    @pl.when(half == 0)
    def _():
        vprime[:, TT:CHUNK, :] = jnp.zeros((NG, CHUNK - TT, NB * GW), bf16)

    shift = mod_ref[0]
    scale1 = 1.0 + mod_ref[1]
    gate = mod_ref[2]
    ng = ng_ref[...]

    def p1(j, c):
        xv = x_ref[pl.ds(pl.multiple_of(j * 8, 8), 8)]
        hv = (xv * _rms_scale(xv) * ng) * scale1[None] + shift[None]
        h_s[pl.ds(pl.multiple_of(j * RB, RB), RB), :] = hv.reshape(RB, D).astype(bf16)
        return c
    lax.fori_loop(0, NRB, p1, 0, unroll=True)

    def proj(c0):
        return jnp.dot(h_s[...], win_ref[:, c0:c0 + D], preferred_element_type=f32)

    xb_s[TAIL:TAIL + TM, :] = proj(C_XB)
    bv[...] = proj(C_V)

    cb = cb_ref[...]
    cw = [cw_ref[k:k + 1, :] for k in range(CONV_W)]

    def p4a(j, c):
        r0 = pl.multiple_of(j * RB, RB)
        xc = cb
        for k in range(CONV_W):
            xc = xc + cw[k] * xb_s[pl.ds(pl.multiple_of(r0 + k * NB, NB), RB), :]
        bxc[pl.ds(r0, RB), :] = xc
        xcb_s[pl.ds(r0, RB), :] = xc.astype(bf16)
        return c
    lax.fori_loop(0, NRB, p4a, 0, unroll=True)

    tail = xb_s[TM:TM + TAIL, :]
    cn_ref[...] = tail
    xb_s[0:TAIL, :] = tail

    bgb[...] = proj(C_GB)
    bu[...] = proj(C_U)

    vng = vng_ref[...]

    def p2(j, c):
        rows = pl.ds(pl.multiple_of(j * RB, RB), RB)
        vv = bv[rows, :]
        vn = vv * _rms_scale(vv) * vng
        for g in range(NG):
            slab[g, rows, :] = vn[:, g * GW:(g + 1) * GW]
        return c
    lax.fori_loop(0, NRB, p2, 0, unroll=True)

    for g in range(NG):
        for b in range(NB):
            vprime[g, pl.ds(hrow, TT), b * GW:(b + 1) * GW] = (
                slab[g, pl.ds(b, TT, stride=NB), :].astype(bf16))

    for q in range(NGB):
        cols = slice(q * GBLK, (q + 1) * GBLK)
        brp[:, cols] = jnp.dot(xcb_s[:, cols], wga_ref[q], preferred_element_type=f32)
        bip[:, cols] = jnp.dot(xcb_s[:, cols], wgx_ref[q], preferred_element_type=f32)

    bga[...] = proj(C_GA)
    bv[...] = proj(C_ZA)

    t_idx = hrow + lax.broadcasted_iota(jnp.int32, (TT, CHUNK), 0)
    s_idx = lax.broadcasted_iota(jnp.int32, (TT, CHUNK), 1)
    causal = s_idx <= t_idx
    for g in range(NG):
        wt = jnp.where(causal, ws_ref[g, pl.ds(hrow, TT), :], jnp.zeros((), bf16))
        sp = jnp.dot(wt, vprime[g], preferred_element_type=f32)
        bias = bs_ref[g, pl.ds(hrow, TT), :]
        for b in range(NB):
            slab[g, pl.ds(b, TT, stride=NB), :] = sp[:, b * GW:(b + 1) * GW] + bias

    bpa[...] = proj(C_ZB)

    def p3(j, c):
        rows = pl.ds(pl.multiple_of(j * RB, RB), RB)
        for g in range(NG):
            cols = slice(g * GW, (g + 1) * GW)
            u = bu[rows, cols]
            ga = bga[rows, cols]
            ya_s[rows, cols] = (u * slab[g, rows, :] * _silu(ga)).astype(bf16)
        return c
    lax.fori_loop(0, NRB, p3, 0, unroll=True)

    bu[...] = jnp.dot(ya_s[...], wpa_ref[...], preferred_element_type=f32)

    ba = brg_ref[0:1, :]
    bx = brg_ref[1:2, :]
    cneg = _neg_c_softplus(lam_ref[...])

    def p4b(j, h):
        rows = pl.ds(pl.multiple_of(j * RB, RB), RB)
        xc = bxc[rows, :]
        r = _sigmoid(brp[rows, :] + ba)
        ig = _sigmoid(bip[rows, :] + bx)
        a = jnp.exp(r * cneg)
        xs = jnp.sqrt(1.0 - a * a) * (ig * xc)
        ys = []
        for t in range(RB // NB):
            h = a[t * NB:(t + 1) * NB] * h + xs[t * NB:(t + 1) * NB]
            ys.append(h)
        yr = jnp.concatenate(ys, axis=0)
        yb_s[rows, :] = (yr * _silu(bgb[rows, :])).astype(bf16)
        return h
    h_fin = lax.fori_loop(0, NRB, p4b, hst[...], unroll=True)
    hst[...] = h_fin
    hl_ref[...] = h_fin

    def p5a(j, c):
        rows = pl.ds(pl.multiple_of(j * RB, RB), RB)
        bu[rows, :] = _sigmoid(bv[rows, :]) * bu[rows, :]
        bpa[rows, :] = _sigmoid(bpa[rows, :])
        return c
    lax.fori_loop(0, NRB, p5a, 0, unroll=True)

    brp[...] = jnp.dot(yb_s[...], wpb_ref[...], preferred_element_type=f32)

    for q in range(NGB):
        cols = slice(q * GBLK, (q + 1) * GBLK)
        for j in range(NRB):
            rows = slice(j * RB, (j + 1) * RB)
            xcb_s[rows, cols] = (bu[rows, cols] + bpa[rows, cols] * brp[rows, cols]
                                 ).astype(bf16)

    bgb[...] = jnp.dot(xcb_s[...], wout_ref[...], preferred_element_type=f32)
    fg = fg_ref[...]

    def p6(j, c):
        ts = pl.ds(pl.multiple_of(j * 8, 8), 8)
        o = bgb[pl.ds(pl.multiple_of(j * RB, RB), RB), :].reshape(8, NB, D)
        y = x_ref[ts] + gate[None] * o
        if final:
            y = y * _rms_scale(y) * fg
        out_ref[ts] = y
        return c
    lax.fori_loop(0, NRB, p6, 0, unroll=True)


def _const_spec(shape):
    nd = len(shape)
    return pl.BlockSpec(shape, lambda i: (0,) * nd, pipeline_mode=pl.Buffered(1))


def _prompt_layer(final, x_tb, mod_l, ng, win, vng, ws, bs, cw, cb, wga, wgx, brg, lam,
                  wpa, wpb, wout, fg):
    consts = (mod_l, ng, win, vng, ws, bs, cw, cb, wga, wgx, brg, lam, wpa, wpb, wout, fg)
    return pl.pallas_call(
        functools.partial(_prompt_layer_kernel, final),
        grid=(NT,),
        in_specs=[pl.BlockSpec((TT, NB, D), lambda i: (i, 0, 0))]
        + [_const_spec(a.shape) for a in consts],
        out_specs=[
            pl.BlockSpec((TT, NB, D), lambda i: (i, 0, 0)),
            pl.BlockSpec((NB, D), lambda i: (0, 0)),
            pl.BlockSpec((TAIL, D), lambda i: (0, 0)),
        ],
        out_shape=[
            jax.ShapeDtypeStruct((SEQ, NB, D), f32),
            jax.ShapeDtypeStruct((NB, D), f32),
            jax.ShapeDtypeStruct((TAIL, D), f32),
        ],
        scratch_shapes=[
            pltpu.VMEM((TM, D), bf16),
            pltpu.VMEM((TM, D), f32),
            pltpu.VMEM((TM, D), f32),
            pltpu.VMEM((TM, D), f32),
            pltpu.VMEM((TM, D), f32),
            pltpu.VMEM((TM, D), f32),
            pltpu.VMEM((TM, D), f32),
            pltpu.VMEM((TM, D), f32),
            pltpu.VMEM((TM, D), f32),
            pltpu.VMEM((NG, TM, GW), f32),
            pltpu.VMEM((NG, CHUNK, NB * GW), bf16),
            pltpu.VMEM((TM + TAIL, D), f32),
            pltpu.VMEM((NB, D), f32),
            pltpu.VMEM((TM, D), bf16),
            pltpu.VMEM((TM, D), bf16),
            pltpu.VMEM((TM, D), bf16),
        ],
        compiler_params=pltpu.CompilerParams(
            dimension_semantics=("arbitrary",), vmem_limit_bytes=VMEM_LIMIT),
        name="prompt_layer_final" if final else "prompt_layer",
    )(x_tb, *consts)


def _sample_kernel(x_ref, mod_ref, h0_ref, cbuf_ref, ng_ref, win_ref, vng_ref, ws0_ref,
                   bs0_ref, cw_ref, cb_ref, wga_ref, wgx_ref, brg_ref, lam_ref,
                   wpa_ref, wpb_ref, wout_ref, fg_ref,
                   y_ref, hs_ref, cs_ref, vs_ref, xs_s):
    l = pl.program_id(0)

    @pl.when(l == 0)
    def _():
        xs_s[...] = x_ref[...]

    x = xs_s[...]
    shift = mod_ref[0, :, 0:D]
    scale = mod_ref[0, :, D:2 * D]
    gate = mod_ref[0, :, 2 * D:3 * D]
    h = ((x * _rms_scale(x) * ng_ref[0]) * (1.0 + scale) + shift).astype(bf16)

    def proj(c0):
        return jnp.dot(h, win_ref[0, :, c0:c0 + D], preferred_element_type=f32)

    v = proj(C_V)
    v = v * _rms_scale(v) * vng_ref[0]
    vs_ref[0] = v
    s = ws0_ref[0] * v + bs0_ref[0]
    ya = (proj(C_U) * s * _silu(proj(C_GA))).astype(bf16)

    xb = proj(C_XB)
    xc = (cb_ref[0] + cw_ref[0, 0:1, :] * cbuf_ref[0, 0] + cw_ref[0, 1:2, :] * cbuf_ref[0, 1]
          + cw_ref[0, 2:3, :] * cbuf_ref[0, 2] + cw_ref[0, 3:4, :] * xb)
    cs_ref[0, 0] = cbuf_ref[0, 1]
    cs_ref[0, 1] = cbuf_ref[0, 2]
    cs_ref[0, 2] = xb
    xcb = xc.astype(bf16)
    rp = jnp.concatenate(
        [jnp.dot(xcb[:, q * GBLK:(q + 1) * GBLK], wga_ref[0, q], preferred_element_type=f32)
         for q in range(NGB)], axis=1)
    ip = jnp.concatenate(
        [jnp.dot(xcb[:, q * GBLK:(q + 1) * GBLK], wgx_ref[0, q], preferred_element_type=f32)
         for q in range(NGB)], axis=1)
    r = _sigmoid(rp + brg_ref[0, 0:1, :])
    ig = _sigmoid(ip + brg_ref[0, 1:2, :])
    a = jnp.exp(r * _neg_c_softplus(lam_ref[0]))
    hn = a * h0_ref[0] + jnp.sqrt(1.0 - a * a) * (ig * xc)
    hs_ref[0] = hn
    yb = (hn * _silu(proj(C_GB))).astype(bf16)

    pa = jnp.dot(ya, wpa_ref[0], preferred_element_type=f32)
    pb = jnp.dot(yb, wpb_ref[0], preferred_element_type=f32)
    mg = (_sigmoid(proj(C_ZA)) * pa + _sigmoid(proj(C_ZB)) * pb).astype(bf16)
    xn = x + gate * jnp.dot(mg, wout_ref[0], preferred_element_type=f32)
    xs_s[...] = xn

    @pl.when(l == DEPTH - 1)
    def _():
        y_ref[...] = xn * _rms_scale(xn) * fg_ref[...]


def _layer_spec(shape):
    nd = len(shape)
    return pl.BlockSpec((1,) + tuple(shape[1:]), lambda l: (l,) + (0,) * (nd - 1))


def _sample_group(x_s, mod_s, h0, cbuf, ng, win, vng, ws0, bs0, cw, cb, wga, wgx, brg, lam,
                  wpa, wpb, wout, fg):
    per_layer = (mod_s, h0, cbuf, ng, win, vng, ws0, bs0, cw, cb, wga, wgx, brg, lam,
                 wpa, wpb, wout)
    return pl.pallas_call(
        _sample_kernel,
        grid=(DEPTH,),
        in_specs=[pl.BlockSpec((NS, D), lambda l: (0, 0))]
        + [_layer_spec(a.shape) for a in per_layer]
        + [pl.BlockSpec((1, D), lambda l: (0, 0))],
        out_specs=[
            pl.BlockSpec((NS, D), lambda l: (0, 0)),
            pl.BlockSpec((1, NS, D), lambda l: (l, 0, 0)),
            pl.BlockSpec((1, CONV_W - 1, NS, D), lambda l: (l, 0, 0, 0)),
            pl.BlockSpec((1, NS, D), lambda l: (l, 0, 0)),
        ],
        out_shape=[
            jax.ShapeDtypeStruct((NS, D), f32),
            jax.ShapeDtypeStruct((DEPTH, NS, D), f32),
            jax.ShapeDtypeStruct((DEPTH, CONV_W - 1, NS, D), f32),
            jax.ShapeDtypeStruct((DEPTH, NS, D), f32),
        ],
        scratch_shapes=[pltpu.VMEM((NS, D), f32)],
        compiler_params=pltpu.CompilerParams(
            dimension_semantics=("arbitrary",), vmem_limit_bytes=VMEM_LIMIT),
        name="sample_layers",
    )(x_s, *per_layer, fg)


def _block_diag_gate(w):
    per = GBLK // BW
    w = w.reshape(DEPTH, NGB, per, BW, BW)
    eye = jnp.eye(per, dtype=w.dtype)
    out = jnp.einsum('lqhij,hk->lqhikj', w, eye)
    return out.reshape(DEPTH, NGB, GBLK, GBLK)


def kernel(x_prompt, x_sample, c_prompt, c_sample, state_rglru_h, state_conv, w_ada, b_ada,
           norm_g, w_in, v_norm_g, w_s, b_s, conv_w, conv_b, w_rg_a, b_rg_a, w_rg_x,
           b_rg_x, lam, w_pa, w_pb, w_out, final_g):
    c_all = jnp.concatenate([c_sample, c_prompt, jnp.zeros((8, D), f32)], axis=0)
    mod = _modulation(c_all, w_ada, b_ada)
    mod_s = mod[:, :NS]
    mod_p = mod[:, NS:NS + NB].reshape(DEPTH, NB, 3, D).transpose(0, 2, 1, 3)

    win = w_in.astype(bf16)
    wpa = w_pa.astype(bf16)
    wpb = w_pb.astype(bf16)
    wout = w_out.astype(bf16)
    wga = _block_diag_gate(w_rg_a).astype(bf16)
    wgx = _block_diag_gate(w_rg_x).astype(bf16)
    ws = w_s.astype(bf16)
    bs = jnp.broadcast_to(b_s[..., None], (DEPTH, NG, CHUNK, GW))
    ng = norm_g.reshape(DEPTH, 1, D)
    vng = v_norm_g.reshape(DEPTH, 1, D)
    cb = conv_b.reshape(DEPTH, 1, D)
    brg = jnp.stack([b_rg_a, b_rg_x], axis=1)
    lam3 = lam.reshape(DEPTH, 1, D)
    fg = final_g.reshape(1, D)

    x_tb = x_prompt.transpose(1, 0, 2)
    hp, cp = [], []
    for l in range(DEPTH):
        x_tb, h_l, c_l = _prompt_layer(
            l == DEPTH - 1, x_tb, mod_p[l], ng[l], win[l], vng[l], ws[l], bs[l],
            conv_w[l], cb[l], wga[l], wgx[l], brg[l], lam3[l], wpa[l], wpb[l], wout[l], fg)
        hp.append(h_l)
        cp.append(c_l.reshape(CONV_W - 1, NB, D).transpose(1, 0, 2))
    y_prompt = x_tb.transpose(1, 0, 2)
    h_prompt = jnp.stack(hp)
    conv_prompt = jnp.stack(cp)

    ws0 = jnp.repeat(w_s[:, :, 0, 0], GW, axis=-1).reshape(DEPTH, 1, D)
    bs0 = jnp.repeat(b_s[:, :, 0], GW, axis=-1).reshape(DEPTH, 1, D)
    cbuf = state_conv.transpose(0, 2, 1, 3)
    y_s, h_sample, cs, vs = _sample_group(
        x_sample.reshape(NS, D), mod_s, state_rglru_h, cbuf, ng, win, vng, ws0, bs0,
        conv_w, cb, wga, wgx, brg, lam3, wpa, wpb, wout, fg)
    y_sample = y_s.reshape(NS, 1, D)
    conv_sample = cs.transpose(0, 2, 1, 3)
    chunk_v_sample = vs.reshape(DEPTH, NS, 1, D)
    return (y_prompt, y_sample, h_prompt, conv_prompt, h_sample, conv_sample,
            chunk_v_sample)
```

```python
import functools

import jax
import jax.numpy as jnp
from jax import lax
from jax.experimental import pallas as pl
from jax.experimental.pallas import tpu as pltpu

D = 1024
NB = 8
SEQ = 2048
DEPTH = 4
NS = 128
CHUNK = 128
GW = 128
NG = D // GW
HB = 16
BW = D // HB
CONV_W = 4
C_RG = 8.0
EPS = 1e-6
D_IN = 7 * D
C_U, C_V, C_GA, C_XB, C_GB, C_ZA, C_ZB = (i * D for i in range(7))

TT = 32
TM = TT * NB
NT = SEQ // TT
RB = 64
NRB = TM // RB
GBLK = 256
NGB = D // GBLK
TAIL = (CONV_W - 1) * NB

VMEM_LIMIT = 62 * 1024 * 1024

f32 = jnp.float32
bf16 = jnp.bfloat16


def _sigmoid(x):
    return jax.nn.sigmoid(x)


def _silu(x):
    return x * jax.nn.sigmoid(x)


def _neg_c_softplus(lam):
    y = -lam
    sp = jnp.maximum(y, 0.0) + jnp.log1p(jnp.exp(-jnp.abs(y)))
    return -C_RG * sp


def _rms_scale(x):
    return lax.rsqrt(jnp.mean(x * x, axis=-1, keepdims=True) + EPS)


def _mod_kernel(c_ref, w_ref, b_ref, o_ref):
    a = _silu(c_ref[...]).astype(bf16)
    w = w_ref[0].astype(bf16)
    o_ref[0] = jnp.dot(a, w, preferred_element_type=f32) + b_ref[0]


def _modulation(c_all, w_ada, b_ada):
    m = c_all.shape[0]
    return pl.pallas_call(
        _mod_kernel,
        grid=(DEPTH, 3),
        in_specs=[
            pl.BlockSpec((m, D), lambda l, j: (0, 0)),
            pl.BlockSpec((1, D, D), lambda l, j: (l, 0, j)),
            pl.BlockSpec((1, 1, D), lambda l, j: (l, 0, j)),
        ],
        out_specs=pl.BlockSpec((1, m, D), lambda l, j: (l, 0, j)),
        out_shape=jax.ShapeDtypeStruct((DEPTH, m, 3 * D), f32),
        compiler_params=pltpu.CompilerParams(
            dimension_semantics=("arbitrary", "arbitrary")),
        name="adaln_mod",
    )(c_all, w_ada, b_ada.reshape(DEPTH, 1, 3 * D))


def _prompt_layer_kernel(final, x_ref, mod_ref, ng_ref, win_ref, vng_ref, ws_ref,
                         bs_ref, cw_ref, cb_ref, wga_ref, wgx_ref, brg_ref, lam_ref,
                         wpa_ref, wpb_ref, wout_ref, fg_ref,
                         out_ref, hl_ref, cn_ref,
                         h_s, bv, bu, bga, bgb, bxc, brp, bip, bpa, slab, vprime, xb_s,
                         hst, ya_s, yb_s, xcb_s):
    i = pl.program_id(0)
    part = i % (CHUNK // TT)
    hrow = pl.multiple_of(part * TT, TT)

    @pl.when(i == 0)
    def _():
        hst[...] = jnp.zeros_like(hst)
        xb_s[0:TAIL, :] = jnp.zeros((TAIL, D), f32)

    @pl.when(part == 0)
    def _():
        vprime[:, TT:CHUNK, :] = jnp.zeros((NG, CHUNK - TT, NB * GW), bf16)

    shift = mod_ref[0]
    scale1 = 1.0 + mod_ref[1]
    gate = mod_ref[2]
    ng = ng_ref[...]

    def p1(j, c):
        xv = x_ref[pl.ds(pl.multiple_of(j * 8, 8), 8)]
        hv = (xv * _rms_scale(xv) * ng) * scale1[None] + shift[None]
        h_s[pl.ds(pl.multiple_of(j * RB, RB), RB), :] = hv.reshape(RB, D).astype(bf16)
        return c
    lax.fori_loop(0, NRB, p1, 0, unroll=True)

    def proj(c0):
        return jnp.dot(h_s[...], win_ref[:, c0:c0 + D], preferred_element_type=f32)

    xb_s[TAIL:TAIL + TM, :] = proj(C_XB)
    bv[...] = proj(C_V)

    cb = cb_ref[...]
    cw = [cw_ref[k:k + 1, :] for k in range(CONV_W)]

    def p4a(j, c):
        r0 = pl.multiple_of(j * RB, RB)
        xc = cb
        for k in range(CONV_W):
            xc = xc + cw[k] * xb_s[pl.ds(pl.multiple_of(r0 + k * NB, NB), RB), :]
        bxc[pl.ds(r0, RB), :] = xc
        xcb_s[pl.ds(r0, RB), :] = xc.astype(bf16)
        return c
    lax.fori_loop(0, NRB, p4a, 0, unroll=True)

    tail = xb_s[TM:TM + TAIL, :]
    cn_ref[...] = tail
    xb_s[0:TAIL, :] = tail

    bgb[...] = proj(C_GB)
    bu[...] = proj(C_U)

    vng = vng_ref[...]

    def p2(j, c):
        rows = pl.ds(pl.multiple_of(j * RB, RB), RB)
        vv = bv[rows, :]
        vn = vv * _rms_scale(vv) * vng
        for g in range(NG):
            slab[g, rows, :] = vn[:, g * GW:(g + 1) * GW]
        return c
    lax.fori_loop(0, NRB, p2, 0, unroll=True)

    for g in range(NG):
        for b in range(NB):
            vprime[g, pl.ds(hrow, TT), b * GW:(b + 1) * GW] = (
                slab[g, pl.ds(b, TT, stride=NB), :].astype(bf16))

    for q in range(NGB):
        cols = slice(q * GBLK, (q + 1) * GBLK)
        brp[:, cols] = jnp.dot(xcb_s[:, cols], wga_ref[q], preferred_element_type=f32)
        bip[:, cols] = jnp.dot(xcb_s[:, cols], wgx_ref[q], preferred_element_type=f32)

    bga[...] = proj(C_GA)
    bv[...] = proj(C_ZA)

    t_idx = hrow + lax.broadcasted_iota(jnp.int32, (TT, CHUNK), 0)
    s_idx = lax.broadcasted_iota(jnp.int32, (TT, CHUNK), 1)
    causal = s_idx <= t_idx
    for g in range(NG):
        wt = jnp.where(causal, ws_ref[g, pl.ds(hrow, TT), :], jnp.zeros((), bf16))
        sp = jnp.dot(wt, vprime[g], preferred_element_type=f32)
        bias = bs_ref[g, pl.ds(hrow, TT), :]
        for b in range(NB):
            slab[g, pl.ds(b, TT, stride=NB), :] = sp[:, b * GW:(b + 1) * GW] + bias

    bpa[...] = proj(C_ZB)

    def p3(j, c):
        rows = pl.ds(pl.multiple_of(j * RB, RB), RB)
        for g in range(NG):
            cols = slice(g * GW, (g + 1) * GW)
            u = bu[rows, cols]
            ga = bga[rows, cols]
            ya_s[rows, cols] = (u * slab[g, rows, :] * _silu(ga)).astype(bf16)
        return c
    lax.fori_loop(0, NRB, p3, 0, unroll=True)

    bu[...] = jnp.dot(ya_s[...], wpa_ref[...], preferred_element_type=f32)

    ba = brg_ref[0:1, :]
    bx = brg_ref[1:2, :]
    cneg = _neg_c_softplus(lam_ref[...])

    def p4b(j, h):
        rows = pl.ds(pl.multiple_of(j * RB, RB), RB)
        xc = bxc[rows, :]
        r = _sigmoid(brp[rows, :] + ba)
        ig = _sigmoid(bip[rows, :] + bx)
        a = jnp.exp(r * cneg)
        xs = jnp.sqrt(1.0 - a * a) * (ig * xc)
        ys = []
        for t in range(RB // NB):
            h = a[t * NB:(t + 1) * NB] * h + xs[t * NB:(t + 1) * NB]
            ys.append(h)
        yr = jnp.concatenate(ys, axis=0)
        yb_s[rows, :] = (yr * _silu(bgb[rows, :])).astype(bf16)
        return h
    h_fin = lax.fori_loop(0, NRB, p4b, hst[...], unroll=True)
    hst[...] = h_fin
    hl_ref[...] = h_fin

    def p5a(j, c):
        rows = pl.ds(pl.multiple_of(j * RB, RB), RB)
        bu[rows, :] = _sigmoid(bv[rows, :]) * bu[rows, :]
        bpa[rows, :] = _sigmoid(bpa[rows, :])
        return c
    lax.fori_loop(0, NRB, p5a, 0, unroll=True)

    brp[...] = jnp.dot(yb_s[...], wpb_ref[...], preferred_element_type=f32)

    for q in range(NGB):
        cols = slice(q * GBLK, (q + 1) * GBLK)
        for j in range(NRB):
            rows = slice(j * RB, (j + 1) * RB)
            xcb_s[rows, cols] = (bu[rows, cols] + bpa[rows, cols] * brp[rows, cols]
                                 ).astype(bf16)

    bgb[...] = jnp.dot(xcb_s[...], wout_ref[...], preferred_element_type=f32)
    fg = fg_ref[...]

    def p6(j, c):
        ts = pl.ds(pl.multiple_of(j * 8, 8), 8)
        o = bgb[pl.ds(pl.multiple_of(j * RB, RB), RB), :].reshape(8, NB, D)
        y = x_ref[ts] + gate[None] * o
        if final:
            y = y * _rms_scale(y) * fg
        out_ref[ts] = y
        return c
    lax.fori_loop(0, NRB, p6, 0, unroll=True)


def _const_spec(shape):
    nd = len(shape)
    return pl.BlockSpec(shape, lambda i: (0,) * nd, pipeline_mode=pl.Buffered(1))


def _prompt_layer(final, x_tb, mod_l, ng, win, vng, ws, bs, cw, cb, wga, wgx, brg, lam,
                  wpa, wpb, wout, fg):
    consts = (mod_l, ng, win, vng, ws, bs, cw, cb, wga, wgx, brg, lam, wpa, wpb, wout, fg)
    return pl.pallas_call(
        functools.partial(_prompt_layer_kernel, final),
        grid=(NT,),
        in_specs=[pl.BlockSpec((TT, NB, D), lambda i: (i, 0, 0))]
        + [_const_spec(a.shape) for a in consts],
        out_specs=[
            pl.BlockSpec((TT, NB, D), lambda i: (i, 0, 0)),
            pl.BlockSpec((NB, D), lambda i: (0, 0)),
            pl.BlockSpec((TAIL, D), lambda i: (0, 0)),
        ],
        out_shape=[
            jax.ShapeDtypeStruct((SEQ, NB, D), f32),
            jax.ShapeDtypeStruct((NB, D), f32),
            jax.ShapeDtypeStruct((TAIL, D), f32),
        ],
        scratch_shapes=[
            pltpu.VMEM((TM, D), bf16),
            pltpu.VMEM((TM, D), f32),
            pltpu.VMEM((TM, D), f32),
            pltpu.VMEM((TM, D), f32),
            pltpu.VMEM((TM, D), f32),
            pltpu.VMEM((TM, D), f32),
            pltpu.VMEM((TM, D), f32),
            pltpu.VMEM((TM, D), f32),
            pltpu.VMEM((TM, D), f32),
            pltpu.VMEM((NG, TM, GW), f32),
            pltpu.VMEM((NG, CHUNK, NB * GW), bf16),
            pltpu.VMEM((TM + TAIL, D), f32),
            pltpu.VMEM((NB, D), f32),
            pltpu.VMEM((TM, D), bf16),
            pltpu.VMEM((TM, D), bf16),
            pltpu.VMEM((TM, D), bf16),
        ],
        compiler_params=pltpu.CompilerParams(
            dimension_semantics=("arbitrary",), vmem_limit_bytes=VMEM_LIMIT),
        name="prompt_layer_final" if final else "prompt_layer",
    )(x_tb, *consts)


def _sample_kernel(x_ref, mod_ref, h0_ref, cbuf_ref, ng_ref, win_ref, vng_ref, ws0_ref,
                   bs0_ref, cw_ref, cb_ref, wga_ref, wgx_ref, brg_ref, lam_ref,
                   wpa_ref, wpb_ref, wout_ref, fg_ref,
                   y_ref, hs_ref, cs_ref, vs_ref, xs_s):
    l = pl.program_id(0)

    @pl.when(l == 0)
    def _():
        xs_s[...] = x_ref[...]

    x = xs_s[...]
    shift = mod_ref[0, :, 0:D]
    scale = mod_ref[0, :, D:2 * D]
    gate = mod_ref[0, :, 2 * D:3 * D]
    h = ((x * _rms_scale(x) * ng_ref[0]) * (1.0 + scale) + shift).astype(bf16)

    def proj(c0):
        return jnp.dot(h, win_ref[0, :, c0:c0 + D], preferred_element_type=f32)

    v = proj(C_V)
    v = v * _rms_scale(v) * vng_ref[0]
    vs_ref[0] = v
    s = ws0_ref[0] * v + bs0_ref[0]
    ya = (proj(C_U) * s * _silu(proj(C_GA))).astype(bf16)

    xb = proj(C_XB)
    xc = (cb_ref[0] + cw_ref[0, 0:1, :] * cbuf_ref[0, 0] + cw_ref[0, 1:2, :] * cbuf_ref[0, 1]
          + cw_ref[0, 2:3, :] * cbuf_ref[0, 2] + cw_ref[0, 3:4, :] * xb)
    cs_ref[0, 0] = cbuf_ref[0, 1]
    cs_ref[0, 1] = cbuf_ref[0, 2]
    cs_ref[0, 2] = xb
    xcb = xc.astype(bf16)
    rp = jnp.concatenate(
        [jnp.dot(xcb[:, q * GBLK:(q + 1) * GBLK], wga_ref[0, q], preferred_element_type=f32)
         for q in range(NGB)], axis=1)
    ip = jnp.concatenate(
        [jnp.dot(xcb[:, q * GBLK:(q + 1) * GBLK], wgx_ref[0, q], preferred_element_type=f32)
         for q in range(NGB)], axis=1)
    r = _sigmoid(rp + brg_ref[0, 0:1, :])
    ig = _sigmoid(ip + brg_ref[0, 1:2, :])
    a = jnp.exp(r * _neg_c_softplus(lam_ref[0]))
    hn = a * h0_ref[0] + jnp.sqrt(1.0 - a * a) * (ig * xc)
    hs_ref[0] = hn
    yb = (hn * _silu(proj(C_GB))).astype(bf16)

    pa = jnp.dot(ya, wpa_ref[0], preferred_element_type=f32)
    pb = jnp.dot(yb, wpb_ref[0], preferred_element_type=f32)
    mg = (_sigmoid(proj(C_ZA)) * pa + _sigmoid(proj(C_ZB)) * pb).astype(bf16)
    xn = x + gate * jnp.dot(mg, wout_ref[0], preferred_element_type=f32)
    xs_s[...] = xn

    @pl.when(l == DEPTH - 1)
    def _():
        y_ref[...] = xn * _rms_scale(xn) * fg_ref[...]


def _layer_spec(shape):
    nd = len(shape)
    return pl.BlockSpec((1,) + tuple(shape[1:]), lambda l: (l,) + (0,) * (nd - 1))


def _sample_group(x_s, mod_s, h0, cbuf, ng, win, vng, ws0, bs0, cw, cb, wga, wgx, brg, lam,
                  wpa, wpb, wout, fg):
    per_layer = (mod_s, h0, cbuf, ng, win, vng, ws0, bs0, cw, cb, wga, wgx, brg, lam,
                 wpa, wpb, wout)
    return pl.pallas_call(
        _sample_kernel,
        grid=(DEPTH,),
        in_specs=[pl.BlockSpec((NS, D), lambda l: (0, 0))]
        + [_layer_spec(a.shape) for a in per_layer]
        + [pl.BlockSpec((1, D), lambda l: (0, 0))],
        out_specs=[
            pl.BlockSpec((NS, D), lambda l: (0, 0)),
            pl.BlockSpec((1, NS, D), lambda l: (l, 0, 0)),
            pl.BlockSpec((1, CONV_W - 1, NS, D), lambda l: (l, 0, 0, 0)),
            pl.BlockSpec((1, NS, D), lambda l: (l, 0, 0)),
        ],
        out_shape=[
            jax.ShapeDtypeStruct((NS, D), f32),
            jax.ShapeDtypeStruct((DEPTH, NS, D), f32),
            jax.ShapeDtypeStruct((DEPTH, CONV_W - 1, NS, D), f32),
            jax.ShapeDtypeStruct((DEPTH, NS, D), f32),
        ],
        scratch_shapes=[pltpu.VMEM((NS, D), f32)],
        compiler_params=pltpu.CompilerParams(
            dimension_semantics=("arbitrary",), vmem_limit_bytes=VMEM_LIMIT),
        name="sample_layers",
    )(x_s, *per_layer, fg)


def _block_diag_gate(w):
    per = GBLK // BW
    w = w.reshape(DEPTH, NGB, per, BW, BW)
    eye = jnp.eye(per, dtype=w.dtype)
    out = jnp.einsum('lqhij,hk->lqhikj', w, eye)
    return out.reshape(DEPTH, NGB, GBLK, GBLK)


def kernel(x_prompt, x_sample, c_prompt, c_sample, state_rglru_h, state_conv, w_ada, b_ada,
           norm_g, w_in, v_norm_g, w_s, b_s, conv_w, conv_b, w_rg_a, b_rg_a, w_rg_x,
           b_rg_x, lam, w_pa, w_pb, w_out, final_g):
    c_all = jnp.concatenate([c_sample, c_prompt, jnp.zeros((8, D), f32)], axis=0)
    mod = _modulation(c_all, w_ada, b_ada)
    mod_s = mod[:, :NS]
    mod_p = mod[:, NS:NS + NB].reshape(DEPTH, NB, 3, D).transpose(0, 2, 1, 3)

    win = w_in.astype(bf16)
    wpa = w_pa.astype(bf16)
    wpb = w_pb.astype(bf16)
    wout = w_out.astype(bf16)
    wga = _block_diag_gate(w_rg_a).astype(bf16)
    wgx = _block_diag_gate(w_rg_x).astype(bf16)
    ws = w_s.astype(bf16)
    bs = jnp.broadcast_to(b_s[..., None], (DEPTH, NG, CHUNK, GW))
    ng = norm_g.reshape(DEPTH, 1, D)
    vng = v_norm_g.reshape(DEPTH, 1, D)
    cb = conv_b.reshape(DEPTH, 1, D)
    brg = jnp.stack([b_rg_a, b_rg_x], axis=1)
    lam3 = lam.reshape(DEPTH, 1, D)
    fg = final_g.reshape(1, D)

    x_tb = x_prompt.transpose(1, 0, 2)
    hp, cp = [], []
    for l in range(DEPTH):
        x_tb, h_l, c_l = _prompt_layer(
            l == DEPTH - 1, x_tb, mod_p[l], ng[l], win[l], vng[l], ws[l], bs[l],
            conv_w[l], cb[l], wga[l], wgx[l], brg[l], lam3[l], wpa[l], wpb[l], wout[l], fg)
        hp.append(h_l)
        cp.append(c_l.reshape(CONV_W - 1, NB, D).transpose(1, 0, 2))
    y_prompt = x_tb.transpose(1, 0, 2)
    h_prompt = jnp.stack(hp)
    conv_prompt = jnp.stack(cp)

    ws0 = jnp.repeat(w_s[:, :, 0, 0], GW, axis=-1).reshape(DEPTH, 1, D)
    bs0 = jnp.repeat(b_s[:, :, 0], GW, axis=-1).reshape(DEPTH, 1, D)
    cbuf = state_conv.transpose(0, 2, 1, 3)
    y_s, h_sample, cs, vs = _sample_group(
        x_sample.reshape(NS, D), mod_s, state_rglru_h, cbuf, ng, win, vng, ws0, bs0,
        conv_w, cb, wga, wgx, brg, lam3, wpa, wpb, wout, fg)
    y_sample = y_s.reshape(NS, 1, D)
    conv_sample = cs.transpose(0, 2, 1, 3)
    chunk_v_sample = vs.reshape(DEPTH, NS, 1, D)
    return (y_prompt, y_sample, h_prompt, conv_prompt, h_sample, conv_sample,
            chunk_v_sample)
```

```python
import functools

import jax
import jax.numpy as jnp
from jax import lax
from jax.experimental import pallas as pl
from jax.experimental.pallas import tpu as pltpu

D = 1024
NB = 8
SEQ = 2048
DEPTH = 4
NS = 128
CHUNK = 128
GW = 128
NG = D // GW
HB = 16
BW = D // HB
CONV_W = 4
C_RG = 8.0
EPS = 1e-6
D_IN = 7 * D
C_U, C_V, C_GA, C_XB, C_GB, C_ZA, C_ZB = (i * D for i in range(7))

TT = 32
TM = TT * NB
NT = SEQ // TT
RB = 64
NRB = TM // RB
GBLK = 256
NGB = D // GBLK
TAIL = (CONV_W - 1) * NB

VMEM_LIMIT = 62 * 1024 * 1024

f32 = jnp.float32
bf16 = jnp.bfloat16


def _sigmoid(x):
    return jax.nn.sigmoid(x)


def _silu(x):
    return x * jax.nn.sigmoid(x)


def _neg_c_softplus(lam):
    y = -lam
    sp = jnp.maximum(y, 0.0) + jnp.log1p(jnp.exp(-jnp.abs(y)))
    return -C_RG * sp


def _rms_scale(x):
    return lax.rsqrt(jnp.mean(x * x, axis=-1, keepdims=True) + EPS)


def _mod_kernel(c_ref, w_ref, b_ref, o_ref):
    a = _silu(c_ref[...]).astype(bf16)
    w = w_ref[0].astype(bf16)
    o_ref[0] = jnp.dot(a, w, preferred_element_type=f32) + b_ref[0]


def _modulation(c_all, w_ada, b_ada):
    m = c_all.shape[0]
    return pl.pallas_call(
        _mod_kernel,
        grid=(DEPTH, 3),
        in_specs=[
            pl.BlockSpec((m, D), lambda l, j: (0, 0)),
            pl.BlockSpec((1, D, D), lambda l, j: (l, 0, j)),
            pl.BlockSpec((1, 1, D), lambda l, j: (l, 0, j)),
        ],
        out_specs=pl.BlockSpec((1, m, D), lambda l, j: (l, 0, j)),
        out_shape=jax.ShapeDtypeStruct((DEPTH, m, 3 * D), f32),
        compiler_params=pltpu.CompilerParams(
            dimension_semantics=("arbitrary", "arbitrary")),
        name="adaln_mod",
    )(c_all, w_ada, b_ada.reshape(DEPTH, 1, 3 * D))


def _prompt_layer_kernel(final, x_ref, xn_ref, mod_ref, ng_ref, win_ref, vng_ref, ws_ref,
                         bs_ref, cw_ref, cb_ref, wga_ref, wgx_ref, brg_ref, lam_ref,
                         wpa_ref, wpb_ref, wout_ref, fg_ref,
                         out_ref, hl_ref, cn_ref,
                         h_s, bv, bu, bga, bgb, bxc, brp, bip, bpa, slab, vprime, xb_s,
                         hst, ya_s, yb_s, xcb_s):
    i = pl.program_id(0)
    part = i % (CHUNK // TT)
    hrow = pl.multiple_of(part * TT, TT)
    slot = i % 2

    shift = mod_ref[0]
    scale1 = 1.0 + mod_ref[1]
    gate = mod_ref[2]
    ng = ng_ref[...]

    def rows_of(j):
        return slice(j * RB, (j + 1) * RB)

    def prenorm(src_ref, dst_slot):
        def unit(j):
            def run():
                xv = src_ref[j * 8:(j + 1) * 8]
                hv = (xv * _rms_scale(xv) * ng) * scale1[None] + shift[None]
                h_s[dst_slot, rows_of(j), :] = hv.reshape(RB, D).astype(bf16)
            return run
        return [unit(j) for j in range(NRB)]

    @pl.when(i == 0)
    def _():
        hst[...] = jnp.zeros_like(hst)
        xb_s[0:TAIL, :] = jnp.zeros((TAIL, D), f32)
        for f in prenorm(x_ref, 0):
            f()

    @pl.when(part == 0)
    def _():
        vprime[:, TT:CHUNK, :] = jnp.zeros((NG, CHUNK - TT, NB * GW), bf16)

    def mm(dst, lhs, w_ref, c0=0, r0=0):
        def chunk(n):
            def run():
                a = h_s[slot] if lhs is None else lhs[...]
                dst[r0:r0 + TM, n * GBLK:(n + 1) * GBLK] = jnp.dot(
                    a, w_ref[:, c0 + n * GBLK:c0 + (n + 1) * GBLK],
                    preferred_element_type=f32)
            return run
        return [chunk(n) for n in range(NGB)]

    def proj(dst, c0, r0=0):
        return mm(dst, None, win_ref, c0, r0)

    def interleave(mxu, vpu):
        n, m = len(mxu), len(vpu)
        done = 0
        for k, f in enumerate(mxu):
            f()
            upto = (m * (k + 1)) // n
            for g in vpu[done:upto]:
                g()
            done = upto
        for g in vpu[done:]:
            g()

    cb = cb_ref[...]
    cw = [cw_ref[k:k + 1, :] for k in range(CONV_W)]

    def p4a(j):
        def run():
            xc = cb
            for k in range(CONV_W):
                xc = xc + cw[k] * xb_s[j * RB + k * NB:(j + 1) * RB + k * NB, :]
            bxc[rows_of(j), :] = xc
            xcb_s[rows_of(j), :] = xc.astype(bf16)
        return run

    def conv_tail():
        tail = xb_s[TM:TM + TAIL, :]
        cn_ref[...] = tail
        xb_s[0:TAIL, :] = tail

    vng = vng_ref[...]

    def p2(j):
        def run():
            vv = bv[rows_of(j), :]
            vn = vv * _rms_scale(vv) * vng
            for g in range(NG):
                slab[g, rows_of(j), :] = vn[:, g * GW:(g + 1) * GW]
        return run

    def relayout(g):
        def run():
            for b in range(NB):
                vprime[g, pl.ds(hrow, TT), b * GW:(b + 1) * GW] = (
                    slab[g, pl.ds(b, TT, stride=NB), :].astype(bf16))
        return run

    def gates(q):
        def run():
            cols = slice(q * GBLK, (q + 1) * GBLK)
            brp[:, cols] = jnp.dot(xcb_s[:, cols], wga_ref[q], preferred_element_type=f32)
            bip[:, cols] = jnp.dot(xcb_s[:, cols], wgx_ref[q], preferred_element_type=f32)
        return run

    t_idx = hrow + lax.broadcasted_iota(jnp.int32, (TT, CHUNK), 0)
    s_idx = lax.broadcasted_iota(jnp.int32, (TT, CHUNK), 1)
    causal = s_idx <= t_idx

    def spatial(g):
        def run():
            wt = jnp.where(causal, ws_ref[g, pl.ds(hrow, TT), :], jnp.zeros((), bf16))
            sp = jnp.dot(wt, vprime[g], preferred_element_type=f32)
            bias = bs_ref[g, pl.ds(hrow, TT), :]
            for b in range(NB):
                slab[g, pl.ds(b, TT, stride=NB), :] = sp[:, b * GW:(b + 1) * GW] + bias
        return run

    def p3(j):
        def run():
            for g in range(NG):
                cols = slice(g * GW, (g + 1) * GW)
                ya_s[rows_of(j), cols] = (
                    bu[rows_of(j), cols] * slab[g, rows_of(j), :]
                    * _silu(bga[rows_of(j), cols])).astype(bf16)
        return run

    ba = brg_ref[0:1, :]
    bx = brg_ref[1:2, :]
    cneg = _neg_c_softplus(lam_ref[...])
    state = [hst[...]]

    def p4b(j):
        def run():
            xc = bxc[rows_of(j), :]
            r = _sigmoid(brp[rows_of(j), :] + ba)
            ig = _sigmoid(bip[rows_of(j), :] + bx)
            a = jnp.exp(r * cneg)
            xs = jnp.sqrt(1.0 - a * a) * (ig * xc)
            h = state[0]
            ys = []
            for t in range(RB // NB):
                h = a[t * NB:(t + 1) * NB] * h + xs[t * NB:(t + 1) * NB]
                ys.append(h)
            state[0] = h
            yr = jnp.concatenate(ys, axis=0)
            yb_s[rows_of(j), :] = (yr * _silu(bgb[rows_of(j), :])).astype(bf16)
        return run

    def scan_done():
        hst[...] = state[0]
        hl_ref[...] = state[0]

    def p5z(j):
        def run():
            bv[rows_of(j), :] = _sigmoid(bv[rows_of(j), :])
            bpa[rows_of(j), :] = _sigmoid(bpa[rows_of(j), :])
        return run

    def p5a(j):
        def run():
            bu[rows_of(j), :] = bv[rows_of(j), :] * bu[rows_of(j), :]
        return run

    def p5b(q):
        def run():
            cols = slice(q * GBLK, (q + 1) * GBLK)
            for j in range(NRB):
                xcb_s[rows_of(j), cols] = (
                    bu[rows_of(j), cols] + bpa[rows_of(j), cols] * brp[rows_of(j), cols]
                ).astype(bf16)
        return run

    fg = fg_ref[...]

    def p6(j):
        def run():
            o = bgb[rows_of(j), :].reshape(8, NB, D)
            y = x_ref[j * 8:(j + 1) * 8] + gate[None] * o
            if final:
                y = y * _rms_scale(y) * fg
            out_ref[j * 8:(j + 1) * 8] = y
        return run

    def units(f):
        return [f(j) for j in range(NRB)]

    interleave(proj(xb_s, C_XB, TAIL), [])
    interleave(proj(bgb, C_GB), units(p4a) + [conv_tail])
    interleave(proj(bu, C_U), [])
    interleave([gates(q) for q in range(NGB)], [])
    interleave(proj(bv, C_V), units(p4b)[:NRB // 2])
    interleave(proj(bga, C_GA), units(p4b)[NRB // 2:] + [scan_done])
    interleave(proj(bpa, C_ZB), units(p2) + [relayout(g) for g in range(NG)])
    interleave([spatial(g) for g in range(NG)], [])
    interleave(proj(bv, C_ZA), units(p3))
    interleave(mm(bu, ya_s, wpa_ref), units(p5z) + prenorm(xn_ref, 1 - slot))
    interleave(mm(brp, yb_s, wpb_ref), units(p5a))
    interleave([p5b(q) for q in range(NGB)], [])
    interleave(mm(bgb, xcb_s, wout_ref), [])
    for f in units(p6):
        f()


def _const_spec(shape):
    nd = len(shape)
    return pl.BlockSpec(shape, lambda i: (0,) * nd, pipeline_mode=pl.Buffered(1))


def _prompt_layer(final, x_tb, mod_l, ng, win, vng, ws, bs, cw, cb, wga, wgx, brg, lam,
                  wpa, wpb, wout, fg):
    consts = (mod_l, ng, win, vng, ws, bs, cw, cb, wga, wgx, brg, lam, wpa, wpb, wout, fg)
    return pl.pallas_call(
        functools.partial(_prompt_layer_kernel, final),
        grid=(NT,),
        in_specs=[pl.BlockSpec((TT, NB, D), lambda i: (i, 0, 0)),
                  pl.BlockSpec((TT, NB, D), lambda i: (jnp.minimum(i + 1, NT - 1), 0, 0))]
        + [_const_spec(a.shape) for a in consts],
        out_specs=[
            pl.BlockSpec((TT, NB, D), lambda i: (i, 0, 0)),
            pl.BlockSpec((NB, D), lambda i: (0, 0)),
            pl.BlockSpec((TAIL, D), lambda i: (0, 0)),
        ],
        out_shape=[
            jax.ShapeDtypeStruct((SEQ, NB, D), f32),
            jax.ShapeDtypeStruct((NB, D), f32),
            jax.ShapeDtypeStruct((TAIL, D), f32),
        ],
        scratch_shapes=[
            pltpu.VMEM((2, TM, D), bf16),
            pltpu.VMEM((TM, D), f32),
            pltpu.VMEM((TM, D), f32),
            pltpu.VMEM((TM, D), f32),
            pltpu.VMEM((TM, D), f32),
            pltpu.VMEM((TM, D), f32),
            pltpu.VMEM((TM, D), f32),
            pltpu.VMEM((TM, D), f32),
            pltpu.VMEM((TM, D), f32),
            pltpu.VMEM((NG, TM, GW), f32),
            pltpu.VMEM((NG, CHUNK, NB * GW), bf16),
            pltpu.VMEM((TM + TAIL, D), f32),
            pltpu.VMEM((NB, D), f32),
            pltpu.VMEM((TM, D), bf16),
            pltpu.VMEM((TM, D), bf16),
            pltpu.VMEM((TM, D), bf16),
        ],
        compiler_params=pltpu.CompilerParams(
            dimension_semantics=("arbitrary",), vmem_limit_bytes=VMEM_LIMIT),
        name="prompt_layer_final" if final else "prompt_layer",
    )(x_tb, x_tb, *consts)


def _sample_kernel(x_ref, mod_ref, h0_ref, cbuf_ref, ng_ref, win_ref, vng_ref, ws0_ref,
                   bs0_ref, cw_ref, cb_ref, wga_ref, wgx_ref, brg_ref, lam_ref,
                   wpa_ref, wpb_ref, wout_ref, fg_ref,
                   y_ref, hs_ref, cs_ref, vs_ref, xs_s):
    l = pl.program_id(0)

    @pl.when(l == 0)
    def _():
        xs_s[...] = x_ref[...]

    x = xs_s[...]
    shift = mod_ref[0, :, 0:D]
    scale = mod_ref[0, :, D:2 * D]
    gate = mod_ref[0, :, 2 * D:3 * D]
    h = ((x * _rms_scale(x) * ng_ref[0]) * (1.0 + scale) + shift).astype(bf16)

    def proj(c0):
        return jnp.dot(h, win_ref[0, :, c0:c0 + D], preferred_element_type=f32)

    v = proj(C_V)
    v = v * _rms_scale(v) * vng_ref[0]
    vs_ref[0] = v
    s = ws0_ref[0] * v + bs0_ref[0]
    ya = (proj(C_U) * s * _silu(proj(C_GA))).astype(bf16)

    xb = proj(C_XB)
    xc = (cb_ref[0] + cw_ref[0, 0:1, :] * cbuf_ref[0, 0] + cw_ref[0, 1:2, :] * cbuf_ref[0, 1]
          + cw_ref[0, 2:3, :] * cbuf_ref[0, 2] + cw_ref[0, 3:4, :] * xb)
    cs_ref[0, 0] = cbuf_ref[0, 1]
    cs_ref[0, 1] = cbuf_ref[0, 2]
    cs_ref[0, 2] = xb
    xcb = xc.astype(bf16)
    rp = jnp.concatenate(
        [jnp.dot(xcb[:, q * GBLK:(q + 1) * GBLK], wga_ref[0, q], preferred_element_type=f32)
         for q in range(NGB)], axis=1)
    ip = jnp.concatenate(
        [jnp.dot(xcb[:, q * GBLK:(q + 1) * GBLK], wgx_ref[0, q], preferred_element_type=f32)
         for q in range(NGB)], axis=1)
    r = _sigmoid(rp + brg_ref[0, 0:1, :])
    ig = _sigmoid(ip + brg_ref[0, 1:2, :])
    a = jnp.exp(r * _neg_c_softplus(lam_ref[0]))
    hn = a * h0_ref[0] + jnp.sqrt(1.0 - a * a) * (ig * xc)
    hs_ref[0] = hn
    yb = (hn * _silu(proj(C_GB))).astype(bf16)

    pa = jnp.dot(ya, wpa_ref[0], preferred_element_type=f32)
    pb = jnp.dot(yb, wpb_ref[0], preferred_element_type=f32)
    mg = (_sigmoid(proj(C_ZA)) * pa + _sigmoid(proj(C_ZB)) * pb).astype(bf16)
    xn = x + gate * jnp.dot(mg, wout_ref[0], preferred_element_type=f32)
    xs_s[...] = xn

    @pl.when(l == DEPTH - 1)
    def _():
        y_ref[...] = xn * _rms_scale(xn) * fg_ref[...]


def _layer_spec(shape):
    nd = len(shape)
    return pl.BlockSpec((1,) + tuple(shape[1:]), lambda l: (l,) + (0,) * (nd - 1))


def _sample_group(x_s, mod_s, h0, cbuf, ng, win, vng, ws0, bs0, cw, cb, wga, wgx, brg, lam,
                  wpa, wpb, wout, fg):
    per_layer = (mod_s, h0, cbuf, ng, win, vng, ws0, bs0, cw, cb, wga, wgx, brg, lam,
                 wpa, wpb, wout)
    return pl.pallas_call(
        _sample_kernel,
        grid=(DEPTH,),
        in_specs=[pl.BlockSpec((NS, D), lambda l: (0, 0))]
        + [_layer_spec(a.shape) for a in per_layer]
        + [pl.BlockSpec((1, D), lambda l: (0, 0))],
        out_specs=[
            pl.BlockSpec((NS, D), lambda l: (0, 0)),
            pl.BlockSpec((1, NS, D), lambda l: (l, 0, 0)),
            pl.BlockSpec((1, CONV_W - 1, NS, D), lambda l: (l, 0, 0, 0)),
            pl.BlockSpec((1, NS, D), lambda l: (l, 0, 0)),
        ],
        out_shape=[
            jax.ShapeDtypeStruct((NS, D), f32),
            jax.ShapeDtypeStruct((DEPTH, NS, D), f32),
            jax.ShapeDtypeStruct((DEPTH, CONV_W - 1, NS, D), f32),
            jax.ShapeDtypeStruct((DEPTH, NS, D), f32),
        ],
        scratch_shapes=[pltpu.VMEM((NS, D), f32)],
        compiler_params=pltpu.CompilerParams(
            dimension_semantics=("arbitrary",), vmem_limit_bytes=VMEM_LIMIT),
        name="sample_layers",
    )(x_s, *per_layer, fg)


def _block_diag_gate(w):
    per = GBLK // BW
    w = w.reshape(DEPTH, NGB, per, BW, BW)
    eye = jnp.eye(per, dtype=w.dtype)
    out = jnp.einsum('lqhij,hk->lqhikj', w, eye)
    return out.reshape(DEPTH, NGB, GBLK, GBLK)


def kernel(x_prompt, x_sample, c_prompt, c_sample, state_rglru_h, state_conv, w_ada, b_ada,
           norm_g, w_in, v_norm_g, w_s, b_s, conv_w, conv_b, w_rg_a, b_rg_a, w_rg_x,
           b_rg_x, lam, w_pa, w_pb, w_out, final_g):
    c_all = jnp.concatenate([c_sample, c_prompt, jnp.zeros((8, D), f32)], axis=0)
    mod = _modulation(c_all, w_ada, b_ada)
    mod_s = mod[:, :NS]
    mod_p = mod[:, NS:NS + NB].reshape(DEPTH, NB, 3, D).transpose(0, 2, 1, 3)

    win = w_in.astype(bf16)
    wpa = w_pa.astype(bf16)
    wpb = w_pb.astype(bf16)
    wout = w_out.astype(bf16)
    wga = _block_diag_gate(w_rg_a).astype(bf16)
    wgx = _block_diag_gate(w_rg_x).astype(bf16)
    ws = w_s.astype(bf16)
    bs = jnp.broadcast_to(b_s[..., None], (DEPTH, NG, CHUNK, GW))
    ng = norm_g.reshape(DEPTH, 1, D)
    vng = v_norm_g.reshape(DEPTH, 1, D)
    cb = conv_b.reshape(DEPTH, 1, D)
    brg = jnp.stack([b_rg_a, b_rg_x], axis=1)
    lam3 = lam.reshape(DEPTH, 1, D)
    fg = final_g.reshape(1, D)

    x_tb = x_prompt.transpose(1, 0, 2)
    hp, cp = [], []
    for l in range(DEPTH):
        x_tb, h_l, c_l = _prompt_layer(
            l == DEPTH - 1, x_tb, mod_p[l], ng[l], win[l], vng[l], ws[l], bs[l],
            conv_w[l], cb[l], wga[l], wgx[l], brg[l], lam3[l], wpa[l], wpb[l], wout[l], fg)
        hp.append(h_l)
        cp.append(c_l.reshape(CONV_W - 1, NB, D).transpose(1, 0, 2))
    y_prompt = x_tb.transpose(1, 0, 2)
    h_prompt = jnp.stack(hp)
    conv_prompt = jnp.stack(cp)

    ws0 = jnp.repeat(w_s[:, :, 0, 0], GW, axis=-1).reshape(DEPTH, 1, D)
    bs0 = jnp.repeat(b_s[:, :, 0], GW, axis=-1).reshape(DEPTH, 1, D)
    cbuf = state_conv.transpose(0, 2, 1, 3)
    y_s, h_sample, cs, vs = _sample_group(
        x_sample.reshape(NS, D), mod_s, state_rglru_h, cbuf, ng, win, vng, ws0, bs0,
        conv_w, cb, wga, wgx, brg, lam3, wpa, wpb, wout, fg)
    y_sample = y_s.reshape(NS, 1, D)
    conv_sample = cs.transpose(0, 2, 1, 3)
    chunk_v_sample = vs.reshape(DEPTH, NS, 1, D)
    return (y_prompt, y_sample, h_prompt, conv_prompt, h_sample, conv_sample,
            chunk_v_sample)
```

```python
import functools

import jax
import jax.numpy as jnp
from jax import lax
from jax.experimental import pallas as pl
from jax.experimental.pallas import tpu as pltpu

D = 1024
NB = 8
SEQ = 2048
DEPTH = 4
NS = 128
CHUNK = 128
GW = 128
NG = D // GW
HB = 16
BW = D // HB
CONV_W = 4
C_RG = 8.0
EPS = 1e-6
D_IN = 7 * D
C_U, C_V, C_GA, C_XB, C_GB, C_ZA, C_ZB = (i * D for i in range(7))

TT = 32
TM = TT * NB
NT = SEQ // TT
RB = 64
NRB = TM // RB
GBLK = 256
NGB = D // GBLK
TAIL = (CONV_W - 1) * NB

VMEM_LIMIT = 62 * 1024 * 1024

f32 = jnp.float32
bf16 = jnp.bfloat16


def _sigmoid(x):
    return jax.nn.sigmoid(x)


def _silu(x):
    return x * jax.nn.sigmoid(x)


def _neg_c_softplus(lam):
    y = -lam
    sp = jnp.maximum(y, 0.0) + jnp.log1p(jnp.exp(-jnp.abs(y)))
    return -C_RG * sp


def _rms_scale(x):
    return lax.rsqrt(jnp.mean(x * x, axis=-1, keepdims=True) + EPS)


def _mod_kernel(c_ref, w_ref, b_ref, o_ref):
    a = _silu(c_ref[...]).astype(bf16)
    w = w_ref[0].astype(bf16)
    o_ref[0] = jnp.dot(a, w, preferred_element_type=f32) + b_ref[0]


def _modulation(c_all, w_ada, b_ada):
    m = c_all.shape[0]
    return pl.pallas_call(
        _mod_kernel,
        grid=(DEPTH, 3),
        in_specs=[
            pl.BlockSpec((m, D), lambda l, j: (0, 0)),
            pl.BlockSpec((1, D, D), lambda l, j: (l, 0, j)),
            pl.BlockSpec((1, 1, D), lambda l, j: (l, 0, j)),
        ],
        out_specs=pl.BlockSpec((1, m, D), lambda l, j: (l, 0, j)),
        out_shape=jax.ShapeDtypeStruct((DEPTH, m, 3 * D), f32),
        compiler_params=pltpu.CompilerParams(
            dimension_semantics=("arbitrary", "arbitrary")),
        name="adaln_mod",
    )(c_all, w_ada, b_ada.reshape(DEPTH, 1, 3 * D))


def _prompt_layer_kernel(final, x_ref, xn_ref, mod_ref, ng_ref, win_ref, vng_ref, ws_ref,
                         bs_ref, cw_ref, cb_ref, wga_ref, wgx_ref, brg_ref, lam_ref,
                         wpa_ref, wpb_ref, wout_ref, fg_ref,
                         out_ref, hl_ref, cn_ref,
                         h_s, bv, bu, bga, bgb, bxc, brp, bip, bpa, slab, vprime, xb_s,
                         hst, ya_s, yb_s, xcb_s):
    i = pl.program_id(0)
    part = i % (CHUNK // TT)
    hrow = pl.multiple_of(part * TT, TT)
    slot = i % 2

    shift = mod_ref[:, 0:D]
    scale1 = 1.0 + mod_ref[:, D:2 * D]
    gate = mod_ref[:, 2 * D:3 * D]
    ng = ng_ref[...]

    def rows_of(j):
        return slice(j * RB, (j + 1) * RB)

    def prenorm(src_ref, dst_slot):
        def unit(j):
            def run():
                xv = src_ref[j * 8:(j + 1) * 8]
                hv = (xv * _rms_scale(xv) * ng) * scale1[None] + shift[None]
                h_s[dst_slot, rows_of(j), :] = hv.reshape(RB, D).astype(bf16)
            return run
        return [unit(j) for j in range(NRB)]

    @pl.when(i == 0)
    def _():
        hst[...] = jnp.zeros_like(hst)
        xb_s[0:TAIL, :] = jnp.zeros((TAIL, D), f32)
        for f in prenorm(x_ref, 0):
            f()

    @pl.when(part == 0)
    def _():
        vprime[:, TT:CHUNK, :] = jnp.zeros((NG, CHUNK - TT, NB * GW), bf16)

    def mm(dst, lhs, w_ref, c0=0, r0=0):
        def chunk(n):
            def run():
                a = h_s[slot] if lhs is None else lhs[...]
                dst[r0:r0 + TM, n * GBLK:(n + 1) * GBLK] = jnp.dot(
                    a, w_ref[:, c0 + n * GBLK:c0 + (n + 1) * GBLK],
                    preferred_element_type=f32)
            return run
        return [chunk(n) for n in range(NGB)]

    def proj(dst, c0, r0=0):
        return mm(dst, None, win_ref, c0, r0)

    def interleave(mxu, vpu):
        n, m = len(mxu), len(vpu)
        done = 0
        for k, f in enumerate(mxu):
            f()
            upto = (m * (k + 1)) // n
            for g in vpu[done:upto]:
                g()
            done = upto
        for g in vpu[done:]:
            g()

    cb = cb_ref[...]
    cw = [cw_ref[k:k + 1, :] for k in range(CONV_W)]

    def p4a(j):
        def run():
            xc = cb
            for k in range(CONV_W):
                xc = xc + cw[k] * xb_s[j * RB + k * NB:(j + 1) * RB + k * NB, :]
            bxc[rows_of(j), :] = xc
            xcb_s[rows_of(j), :] = xc.astype(bf16)
        return run

    def conv_tail():
        tail = xb_s[TM:TM + TAIL, :]
        cn_ref[...] = tail
        xb_s[0:TAIL, :] = tail

    vng = vng_ref[...]

    def p2(j):
        def run():
            vv = bv[rows_of(j), :]
            vn = vv * _rms_scale(vv) * vng
            for g in range(NG):
                slab[g, rows_of(j), :] = vn[:, g * GW:(g + 1) * GW]
        return run

    def relayout(g):
        def run():
            for b in range(NB):
                vprime[g, pl.ds(hrow, TT), b * GW:(b + 1) * GW] = (
                    slab[g, pl.ds(b, TT, stride=NB), :].astype(bf16))
        return run

    def gates(q):
        def run():
            cols = slice(q * GBLK, (q + 1) * GBLK)
            brp[:, cols] = jnp.dot(xcb_s[:, cols], wga_ref[q], preferred_element_type=f32)
            bip[:, cols] = jnp.dot(xcb_s[:, cols], wgx_ref[q], preferred_element_type=f32)
        return run

    t_idx = hrow + lax.broadcasted_iota(jnp.int32, (TT, CHUNK), 0)
    s_idx = lax.broadcasted_iota(jnp.int32, (TT, CHUNK), 1)
    causal = s_idx <= t_idx

    def spatial(g):
        def run():
            wt = jnp.where(causal, ws_ref[g, pl.ds(hrow, TT), :], jnp.zeros((), bf16))
            sp = jnp.dot(wt, vprime[g], preferred_element_type=f32)
            bias = bs_ref[g, pl.ds(hrow, TT), :]
            for b in range(NB):
                slab[g, pl.ds(b, TT, stride=NB), :] = sp[:, b * GW:(b + 1) * GW] + bias
        return run

    def p3(j):
        def run():
            for g in range(NG):
                cols = slice(g * GW, (g + 1) * GW)
                ya_s[rows_of(j), cols] = (
                    bu[rows_of(j), cols] * slab[g, rows_of(j), :]
                    * _silu(bga[rows_of(j), cols])).astype(bf16)
        return run

    ba = brg_ref[0:1, :]
    bx = brg_ref[1:2, :]
    cneg = _neg_c_softplus(lam_ref[...])
    state = [hst[...]]

    def p4b(j):
        def run():
            xc = bxc[rows_of(j), :]
            r = _sigmoid(brp[rows_of(j), :] + ba)
            ig = _sigmoid(bip[rows_of(j), :] + bx)
            a = jnp.exp(r * cneg)
            xs = jnp.sqrt(1.0 - a * a) * (ig * xc)
            h = state[0]
            ys = []
            for t in range(RB // NB):
                h = a[t * NB:(t + 1) * NB] * h + xs[t * NB:(t + 1) * NB]
                ys.append(h)
            state[0] = h
            yr = jnp.concatenate(ys, axis=0)
            yb_s[rows_of(j), :] = (yr * _silu(bgb[rows_of(j), :])).astype(bf16)
        return run

    def scan_done():
        hst[...] = state[0]
        hl_ref[...] = state[0]

    def p5z(j):
        def run():
            bv[rows_of(j), :] = _sigmoid(bv[rows_of(j), :])
            bpa[rows_of(j), :] = _sigmoid(bpa[rows_of(j), :])
        return run

    def p5a(j):
        def run():
            bu[rows_of(j), :] = bv[rows_of(j), :] * bu[rows_of(j), :]
        return run

    def p5b(q):
        def run():
            cols = slice(q * GBLK, (q + 1) * GBLK)
            for j in range(NRB):
                xcb_s[rows_of(j), cols] = (
                    bu[rows_of(j), cols] + bpa[rows_of(j), cols] * brp[rows_of(j), cols]
                ).astype(bf16)
        return run

    fg = fg_ref[...]

    def p6(j):
        def run():
            o = bgb[rows_of(j), :].reshape(8, NB, D)
            y = x_ref[j * 8:(j + 1) * 8] + gate[None] * o
            if final:
                y = y * _rms_scale(y) * fg
            out_ref[j * 8:(j + 1) * 8] = y
        return run

    def units(f):
        return [f(j) for j in range(NRB)]

    interleave(proj(xb_s, C_XB, TAIL), [])
    interleave(proj(bgb, C_GB), units(p4a) + [conv_tail])
    interleave(proj(bu, C_U), [])
    interleave([gates(q) for q in range(NGB)], [])
    interleave(proj(bv, C_V), units(p4b)[:NRB // 2])
    interleave(proj(bga, C_GA), units(p4b)[NRB // 2:] + [scan_done])
    interleave(proj(bpa, C_ZB), units(p2) + [relayout(g) for g in range(NG)])
    interleave([spatial(g) for g in range(NG)], [])
    interleave(proj(bv, C_ZA), units(p3))
    interleave(mm(bu, ya_s, wpa_ref), units(p5z) + prenorm(xn_ref, 1 - slot))
    interleave(mm(brp, yb_s, wpb_ref), units(p5a))
    interleave([p5b(q) for q in range(NGB)], [])
    interleave(mm(bgb, xcb_s, wout_ref), [])
    for f in units(p6):
        f()


def _layer_const_spec(shape, l):
    nd = len(shape)
    return pl.BlockSpec((None,) + tuple(shape[1:]), lambda i: (l,) + (0,) * (nd - 1),
                        pipeline_mode=pl.Buffered(1))


def _prompt_layer(l, x_tb, mod, ng, win, vng, ws, bs, cw, cb, wga, wgx, brg, lam,
                  wpa, wpb, wout, fg):
    final = l == DEPTH - 1
    stacked = (ng, win, vng, ws, bs, cw, cb, wga, wgx, brg, lam, wpa, wpb, wout)
    mod_spec = pl.BlockSpec((None, NB, 3 * D), lambda i: (l, NS // NB, 0),
                            pipeline_mode=pl.Buffered(1))
    fg_spec = pl.BlockSpec((1, D), lambda i: (0, 0), pipeline_mode=pl.Buffered(1))
    return pl.pallas_call(
        functools.partial(_prompt_layer_kernel, final),
        grid=(NT,),
        in_specs=[pl.BlockSpec((TT, NB, D), lambda i: (i, 0, 0)),
                  pl.BlockSpec((TT, NB, D), lambda i: (jnp.minimum(i + 1, NT - 1), 0, 0))]
        + [mod_spec] + [_layer_const_spec(a.shape, l) for a in stacked] + [fg_spec],
        out_specs=[
            pl.BlockSpec((TT, NB, D), lambda i: (i, 0, 0)),
            pl.BlockSpec((NB, D), lambda i: (0, 0)),
            pl.BlockSpec((TAIL, D), lambda i: (0, 0)),
        ],
        out_shape=[
            jax.ShapeDtypeStruct((SEQ, NB, D), f32),
            jax.ShapeDtypeStruct((NB, D), f32),
            jax.ShapeDtypeStruct((TAIL, D), f32),
        ],
        scratch_shapes=[
            pltpu.VMEM((2, TM, D), bf16),
            pltpu.VMEM((TM, D), f32),
            pltpu.VMEM((TM, D), f32),
            pltpu.VMEM((TM, D), f32),
            pltpu.VMEM((TM, D), f32),
            pltpu.VMEM((TM, D), f32),
            pltpu.VMEM((TM, D), f32),
            pltpu.VMEM((TM, D), f32),
            pltpu.VMEM((TM, D), f32),
            pltpu.VMEM((NG, TM, GW), f32),
            pltpu.VMEM((NG, CHUNK, NB * GW), bf16),
            pltpu.VMEM((TM + TAIL, D), f32),
            pltpu.VMEM((NB, D), f32),
            pltpu.VMEM((TM, D), bf16),
            pltpu.VMEM((TM, D), bf16),
            pltpu.VMEM((TM, D), bf16),
        ],
        compiler_params=pltpu.CompilerParams(
            dimension_semantics=("arbitrary",), vmem_limit_bytes=VMEM_LIMIT),
        name="prompt_layer_final" if final else "prompt_layer",
    )(x_tb, x_tb, mod, *stacked, fg)


def _sample_kernel(x_ref, mod_ref, h0_ref, cbuf_ref, ng_ref, win_ref, vng_ref, ws0_ref,
                   bs0_ref, cw_ref, cb_ref, wga_ref, wgx_ref, brg_ref, lam_ref,
                   wpa_ref, wpb_ref, wout_ref, fg_ref,
                   y_ref, hs_ref, cs_ref, vs_ref, xs_s):
    l = pl.program_id(0)

    @pl.when(l == 0)
    def _():
        xs_s[...] = x_ref[...]

    x = xs_s[...]
    shift = mod_ref[0, :, 0:D]
    scale = mod_ref[0, :, D:2 * D]
    gate = mod_ref[0, :, 2 * D:3 * D]
    h = ((x * _rms_scale(x) * ng_ref[0]) * (1.0 + scale) + shift).astype(bf16)

    def proj(c0):
        return jnp.dot(h, win_ref[0, :, c0:c0 + D], preferred_element_type=f32)

    v = proj(C_V)
    v = v * _rms_scale(v) * vng_ref[0]
    vs_ref[0] = v
    s = ws0_ref[0] * v + bs0_ref[0]
    ya = (proj(C_U) * s * _silu(proj(C_GA))).astype(bf16)

    xb = proj(C_XB)
    xc = (cb_ref[0] + cw_ref[0, 0:1, :] * cbuf_ref[0, 0] + cw_ref[0, 1:2, :] * cbuf_ref[0, 1]
          + cw_ref[0, 2:3, :] * cbuf_ref[0, 2] + cw_ref[0, 3:4, :] * xb)
    cs_ref[0, 0] = cbuf_ref[0, 1]
    cs_ref[0, 1] = cbuf_ref[0, 2]
    cs_ref[0, 2] = xb
    xcb = xc.astype(bf16)
    rp = jnp.concatenate(
        [jnp.dot(xcb[:, q * GBLK:(q + 1) * GBLK], wga_ref[0, q], preferred_element_type=f32)
         for q in range(NGB)], axis=1)
    ip = jnp.concatenate(
        [jnp.dot(xcb[:, q * GBLK:(q + 1) * GBLK], wgx_ref[0, q], preferred_element_type=f32)
         for q in range(NGB)], axis=1)
    r = _sigmoid(rp + brg_ref[0, 0:1, :])
    ig = _sigmoid(ip + brg_ref[0, 1:2, :])
    a = jnp.exp(r * _neg_c_softplus(lam_ref[0]))
    hn = a * h0_ref[0] + jnp.sqrt(1.0 - a * a) * (ig * xc)
    hs_ref[0] = hn
    yb = (hn * _silu(proj(C_GB))).astype(bf16)

    pa = jnp.dot(ya, wpa_ref[0], preferred_element_type=f32)
    pb = jnp.dot(yb, wpb_ref[0], preferred_element_type=f32)
    mg = (_sigmoid(proj(C_ZA)) * pa + _sigmoid(proj(C_ZB)) * pb).astype(bf16)
    xn = x + gate * jnp.dot(mg, wout_ref[0], preferred_element_type=f32)
    xs_s[...] = xn

    @pl.when(l == DEPTH - 1)
    def _():
        y_ref[...] = xn * _rms_scale(xn) * fg_ref[...]


def _layer_spec(shape):
    nd = len(shape)
    return pl.BlockSpec((1,) + tuple(shape[1:]), lambda l: (l,) + (0,) * (nd - 1))


def _sample_group(x_s, mod_s, h0, cbuf, ng, win, vng, ws0, bs0, cw, cb, wga, wgx, brg, lam,
                  wpa, wpb, wout, fg):
    per_layer = (mod_s, h0, cbuf, ng, win, vng, ws0, bs0, cw, cb, wga, wgx, brg, lam,
                 wpa, wpb, wout)
    return pl.pallas_call(
        _sample_kernel,
        grid=(DEPTH,),
        in_specs=[pl.BlockSpec((NS, D), lambda l: (0, 0)),
                  pl.BlockSpec((1, NS, 3 * D), lambda l: (l, 0, 0))]
        + [_layer_spec(a.shape) for a in per_layer[1:]]
        + [pl.BlockSpec((1, D), lambda l: (0, 0))],
        out_specs=[
            pl.BlockSpec((NS, D), lambda l: (0, 0)),
            pl.BlockSpec((1, NS, D), lambda l: (l, 0, 0)),
            pl.BlockSpec((1, CONV_W - 1, NS, D), lambda l: (l, 0, 0, 0)),
            pl.BlockSpec((1, NS, D), lambda l: (l, 0, 0)),
        ],
        out_shape=[
            jax.ShapeDtypeStruct((NS, D), f32),
            jax.ShapeDtypeStruct((DEPTH, NS, D), f32),
            jax.ShapeDtypeStruct((DEPTH, CONV_W - 1, NS, D), f32),
            jax.ShapeDtypeStruct((DEPTH, NS, D), f32),
        ],
        scratch_shapes=[pltpu.VMEM((NS, D), f32)],
        compiler_params=pltpu.CompilerParams(
            dimension_semantics=("arbitrary",), vmem_limit_bytes=VMEM_LIMIT),
        name="sample_layers",
    )(x_s, *per_layer, fg)


def _block_diag_gate(w):
    per = GBLK // BW
    w = w.reshape(DEPTH, NGB, per, BW, BW)
    eye = jnp.eye(per, dtype=w.dtype)
    out = jnp.einsum('lqhij,hk->lqhikj', w, eye)
    return out.reshape(DEPTH, NGB, GBLK, GBLK)


def kernel(x_prompt, x_sample, c_prompt, c_sample, state_rglru_h, state_conv, w_ada, b_ada,
           norm_g, w_in, v_norm_g, w_s, b_s, conv_w, conv_b, w_rg_a, b_rg_a, w_rg_x,
           b_rg_x, lam, w_pa, w_pb, w_out, final_g):
    c_all = jnp.concatenate([c_sample, c_prompt, jnp.zeros((8, D), f32)], axis=0)
    mod = _modulation(c_all, w_ada, b_ada)

    win = w_in.astype(bf16)
    wpa = w_pa.astype(bf16)
    wpb = w_pb.astype(bf16)
    wout = w_out.astype(bf16)
    wga = _block_diag_gate(w_rg_a).astype(bf16)
    wgx = _block_diag_gate(w_rg_x).astype(bf16)
    ws = w_s.astype(bf16)
    bs = jnp.broadcast_to(b_s[..., None], (DEPTH, NG, CHUNK, GW))
    ng = norm_g.reshape(DEPTH, 1, D)
    vng = v_norm_g.reshape(DEPTH, 1, D)
    cb = conv_b.reshape(DEPTH, 1, D)
    brg = jnp.stack([b_rg_a, b_rg_x], axis=1)
    lam3 = lam.reshape(DEPTH, 1, D)
    fg = final_g.reshape(1, D)

    x_tb = x_prompt.transpose(1, 0, 2)
    hp, cp = [], []
    for l in range(DEPTH):
        x_tb, h_l, c_l = _prompt_layer(
            l, x_tb, mod, ng, win, vng, ws, bs, conv_w, cb, wga, wgx, brg, lam3,
            wpa, wpb, wout, fg)
        hp.append(h_l)
        cp.append(c_l.reshape(CONV_W - 1, NB, D).transpose(1, 0, 2))
    y_prompt = x_tb.transpose(1, 0, 2)
    h_prompt = jnp.stack(hp)
    conv_prompt = jnp.stack(cp)

    ws0 = jnp.repeat(w_s[:, :, 0, 0], GW, axis=-1).reshape(DEPTH, 1, D)
    bs0 = jnp.repeat(b_s[:, :, 0], GW, axis=-1).reshape(DEPTH, 1, D)
    cbuf = state_conv.transpose(0, 2, 1, 3)
    y_s, h_sample, cs, vs = _sample_group(
        x_sample.reshape(NS, D), mod, state_rglru_h, cbuf, ng, win, vng, ws0, bs0,
        conv_w, cb, wga, wgx, brg, lam3, wpa, wpb, wout, fg)
    y_sample = y_s.reshape(NS, 1, D)
    conv_sample = cs.transpose(0, 2, 1, 3)
    chunk_v_sample = vs.reshape(DEPTH, NS, 1, D)
    return (y_prompt, y_sample, h_prompt, conv_prompt, h_sample, conv_sample,
            chunk_v_sample)
```

```python
import functools

import jax
import jax.numpy as jnp
from jax import lax
from jax.experimental import pallas as pl
from jax.experimental.pallas import tpu as pltpu

D = 1024
NB = 8
SEQ = 2048
DEPTH = 4
NS = 128
CHUNK = 128
GW = 128
NG = D // GW
HB = 16
BW = D // HB
CONV_W = 4
C_RG = 8.0
EPS = 1e-6
D_IN = 7 * D
C_U, C_V, C_GA, C_XB, C_GB, C_ZA, C_ZB = (i * D for i in range(7))

TT = 32
TM = TT * NB
NT = SEQ // TT
RB = 64
NRB = TM // RB
GBLK = 256
NGB = D // GBLK
TAIL = (CONV_W - 1) * NB

VMEM_LIMIT = 62 * 1024 * 1024

f32 = jnp.float32
bf16 = jnp.bfloat16


def _sigmoid(x):
    return jax.nn.sigmoid(x)


def _silu(x):
    return x * jax.nn.sigmoid(x)


def _neg_c_softplus(lam):
    y = -lam
    sp = jnp.maximum(y, 0.0) + jnp.log1p(jnp.exp(-jnp.abs(y)))
    return -C_RG * sp


def _rms_scale(x):
    return lax.rsqrt(jnp.mean(x * x, axis=-1, keepdims=True) + EPS)


def _mod_kernel(c_ref, w_ref, b_ref, o_ref):
    a = _silu(c_ref[...]).astype(bf16)
    w = w_ref[0].astype(bf16)
    o_ref[0] = jnp.dot(a, w, preferred_element_type=f32) + b_ref[0]


def _modulation(c_all, w_ada, b_ada):
    m = c_all.shape[0]
    return pl.pallas_call(
        _mod_kernel,
        grid=(DEPTH, 3),
        in_specs=[
            pl.BlockSpec((m, D), lambda l, j: (0, 0)),
            pl.BlockSpec((1, D, D), lambda l, j: (l, 0, j)),
            pl.BlockSpec((1, 1, D), lambda l, j: (l, 0, j)),
        ],
        out_specs=pl.BlockSpec((1, m, D), lambda l, j: (l, 0, j)),
        out_shape=jax.ShapeDtypeStruct((DEPTH, m, 3 * D), f32),
        compiler_params=pltpu.CompilerParams(
            dimension_semantics=("arbitrary", "arbitrary")),
        name="adaln_mod",
    )(c_all, w_ada, b_ada.reshape(DEPTH, 1, 3 * D))


def _prompt_layer_kernel(final, x_ref, xn_ref, mod_ref, ng_ref, wu_ref, wv_ref, wga_in_ref,
                         wxb_ref, wgb_ref, wza_ref, wzb_ref, vng_ref, ws_ref,
                         bs_ref, cw_ref, cb_ref, wga_ref, wgx_ref, brg_ref, lam_ref,
                         wpa_ref, wpb_ref, wout_ref, fg_ref,
                         out_ref, hl_ref, cn_ref,
                         h_s, bv, bu, bga, bgb, bxc, brp, bip, bpa, slab, vprime, xb_s,
                         hst, ya_s, yb_s, xcb_s):
    wins = (wu_ref, wv_ref, wga_in_ref, wxb_ref, wgb_ref, wza_ref, wzb_ref)
    i = pl.program_id(0)
    part = i % (CHUNK // TT)
    hrow = pl.multiple_of(part * TT, TT)
    slot = i % 2

    shift = mod_ref[:, 0:D]
    scale1 = 1.0 + mod_ref[:, D:2 * D]
    gate = mod_ref[:, 2 * D:3 * D]
    ng = ng_ref[...]

    def rows_of(j):
        return slice(j * RB, (j + 1) * RB)

    def prenorm(src_ref, dst_slot):
        def unit(j):
            def run():
                xv = src_ref[j * 8:(j + 1) * 8]
                hv = (xv * _rms_scale(xv) * ng) * scale1[None] + shift[None]
                h_s[dst_slot, rows_of(j), :] = hv.reshape(RB, D).astype(bf16)
            return run
        return [unit(j) for j in range(NRB)]

    @pl.when(i == 0)
    def _():
        hst[...] = jnp.zeros_like(hst)
        xb_s[0:TAIL, :] = jnp.zeros((TAIL, D), f32)
        for f in prenorm(x_ref, 0):
            f()

    @pl.when(part == 0)
    def _():
        vprime[:, TT:CHUNK, :] = jnp.zeros((NG, CHUNK - TT, NB * GW), bf16)

    def mm(dst, lhs, w_ref, c0=0, r0=0):
        def chunk(n):
            def run():
                a = h_s[slot] if lhs is None else lhs[...]
                dst[r0:r0 + TM, n * GBLK:(n + 1) * GBLK] = jnp.dot(
                    a, w_ref[:, c0 + n * GBLK:c0 + (n + 1) * GBLK],
                    preferred_element_type=f32)
            return run
        return [chunk(n) for n in range(NGB)]

    def proj(dst, c0, r0=0):
        return mm(dst, None, wins[c0 // D], 0, r0)

    def interleave(mxu, vpu):
        n, m = len(mxu), len(vpu)
        done = 0
        for k, f in enumerate(mxu):
            f()
            upto = (m * (k + 1)) // n
            for g in vpu[done:upto]:
                g()
            done = upto
        for g in vpu[done:]:
            g()

    cb = cb_ref[...]
    cw = [cw_ref[k:k + 1, :] for k in range(CONV_W)]

    def p4a(j):
        def run():
            xc = cb
            for k in range(CONV_W):
                xc = xc + cw[k] * xb_s[j * RB + k * NB:(j + 1) * RB + k * NB, :]
            bxc[rows_of(j), :] = xc
            xcb_s[rows_of(j), :] = xc.astype(bf16)
        return run

    def conv_tail():
        tail = xb_s[TM:TM + TAIL, :]
        cn_ref[...] = tail
        xb_s[0:TAIL, :] = tail

    vng = vng_ref[...]

    def p2(j):
        def run():
            vv = bv[rows_of(j), :]
            vn = vv * _rms_scale(vv) * vng
            for g in range(NG):
                slab[g, rows_of(j), :] = vn[:, g * GW:(g + 1) * GW]
        return run

    def relayout(g):
        def run():
            for b in range(NB):
                vprime[g, pl.ds(hrow, TT), b * GW:(b + 1) * GW] = (
                    slab[g, pl.ds(b, TT, stride=NB), :].astype(bf16))
        return run

    def gates(q):
        def run():
            cols = slice(q * GBLK, (q + 1) * GBLK)
            brp[:, cols] = jnp.dot(xcb_s[:, cols], wga_ref[q], preferred_element_type=f32)
            bip[:, cols] = jnp.dot(xcb_s[:, cols], wgx_ref[q], preferred_element_type=f32)
        return run

    t_idx = hrow + lax.broadcasted_iota(jnp.int32, (TT, CHUNK), 0)
    s_idx = lax.broadcasted_iota(jnp.int32, (TT, CHUNK), 1)
    causal = s_idx <= t_idx

    def spatial(g):
        def run():
            wt = jnp.where(causal, ws_ref[g, pl.ds(hrow, TT), :], jnp.zeros((), bf16))
            sp = jnp.dot(wt, vprime[g], preferred_element_type=f32)
            bias = bs_ref[g, pl.ds(hrow, TT), :]
            for b in range(NB):
                slab[g, pl.ds(b, TT, stride=NB), :] = sp[:, b * GW:(b + 1) * GW] + bias
        return run

    def p3(j):
        def run():
            for g in range(NG):
                cols = slice(g * GW, (g + 1) * GW)
                ya_s[rows_of(j), cols] = (
                    bu[rows_of(j), cols] * slab[g, rows_of(j), :]
                    * _silu(bga[rows_of(j), cols])).astype(bf16)
        return run

    ba = brg_ref[0:1, :]
    bx = brg_ref[1:2, :]
    cneg = _neg_c_softplus(lam_ref[...])
    state = [hst[...]]

    def p4b(j):
        def run():
            xc = bxc[rows_of(j), :]
            r = _sigmoid(brp[rows_of(j), :] + ba)
            ig = _sigmoid(bip[rows_of(j), :] + bx)
            a = jnp.exp(r * cneg)
            xs = jnp.sqrt(1.0 - a * a) * (ig * xc)
            h = state[0]
            ys = []
            for t in range(RB // NB):
                h = a[t * NB:(t + 1) * NB] * h + xs[t * NB:(t + 1) * NB]
                ys.append(h)
            state[0] = h
            yr = jnp.concatenate(ys, axis=0)
            yb_s[rows_of(j), :] = (yr * _silu(bgb[rows_of(j), :])).astype(bf16)
        return run

    def scan_done():
        hst[...] = state[0]
        hl_ref[...] = state[0]

    def p5z(j):
        def run():
            bv[rows_of(j), :] = _sigmoid(bv[rows_of(j), :])
            bpa[rows_of(j), :] = _sigmoid(bpa[rows_of(j), :])
        return run

    def p5a(j):
        def run():
            bu[rows_of(j), :] = bv[rows_of(j), :] * bu[rows_of(j), :]
        return run

    def p5b(q):
        def run():
            cols = slice(q * GBLK, (q + 1) * GBLK)
            for j in range(NRB):
                xcb_s[rows_of(j), cols] = (
                    bu[rows_of(j), cols] + bpa[rows_of(j), cols] * brp[rows_of(j), cols]
                ).astype(bf16)
        return run

    fg = fg_ref[...]

    def p6(j):
        def run():
            o = bgb[rows_of(j), :].reshape(8, NB, D)
            y = x_ref[j * 8:(j + 1) * 8] + gate[None] * o
            if final:
                y = y * _rms_scale(y) * fg
            out_ref[j * 8:(j + 1) * 8] = y
        return run

    def units(f):
        return [f(j) for j in range(NRB)]

    interleave(proj(xb_s, C_XB, TAIL), [])
    interleave(proj(bgb, C_GB), units(p4a) + [conv_tail])
    interleave(proj(bu, C_U), [])
    interleave([gates(q) for q in range(NGB)], [])
    interleave(proj(bv, C_V), units(p4b)[:NRB // 2])
    interleave(proj(bga, C_GA), units(p4b)[NRB // 2:] + [scan_done])
    interleave(proj(bpa, C_ZB), units(p2) + [relayout(g) for g in range(NG)])
    interleave([spatial(g) for g in range(NG)], [])
    interleave(proj(bv, C_ZA), units(p3))
    interleave(mm(bu, ya_s, wpa_ref), units(p5z) + prenorm(xn_ref, 1 - slot))
    interleave(mm(brp, yb_s, wpb_ref), units(p5a))
    interleave([p5b(q) for q in range(NGB)], [])
    interleave(mm(bgb, xcb_s, wout_ref), [])
    for f in units(p6):
        f()


def _layer_const_spec(shape, l):
    nd = len(shape)
    return pl.BlockSpec((None,) + tuple(shape[1:]), lambda i: (l,) + (0,) * (nd - 1),
                        pipeline_mode=pl.Buffered(1))


def _prompt_layer(l, x_tb, mod, ng, win, vng, ws, bs, cw, cb, wga, wgx, brg, lam,
                  wpa, wpb, wout, fg):
    final = l == DEPTH - 1
    stacked = (vng, ws, bs, cw, cb, wga, wgx, brg, lam, wpa, wpb, wout)
    mod_spec = pl.BlockSpec((None, NB, 3 * D), lambda i: (l, NS // NB, 0),
                            pipeline_mode=pl.Buffered(1))
    fg_spec = pl.BlockSpec((1, D), lambda i: (0, 0), pipeline_mode=pl.Buffered(1))
    win_specs = [pl.BlockSpec((None, D, D), lambda i, c=c: (l, 0, c),
                              pipeline_mode=pl.Buffered(1)) for c in range(D_IN // D)]
    return pl.pallas_call(
        functools.partial(_prompt_layer_kernel, final),
        grid=(NT,),
        in_specs=[pl.BlockSpec((TT, NB, D), lambda i: (i, 0, 0)),
                  pl.BlockSpec((TT, NB, D), lambda i: (jnp.minimum(i + 1, NT - 1), 0, 0))]
        + [mod_spec, _layer_const_spec(ng.shape, l)] + win_specs
        + [_layer_const_spec(a.shape, l) for a in stacked] + [fg_spec],
        out_specs=[
            pl.BlockSpec((TT, NB, D), lambda i: (i, 0, 0)),
            pl.BlockSpec((NB, D), lambda i: (0, 0)),
            pl.BlockSpec((TAIL, D), lambda i: (0, 0)),
        ],
        out_shape=[
            jax.ShapeDtypeStruct((SEQ, NB, D), f32),
            jax.ShapeDtypeStruct((NB, D), f32),
            jax.ShapeDtypeStruct((TAIL, D), f32),
        ],
        scratch_shapes=[
            pltpu.VMEM((2, TM, D), bf16),
            pltpu.VMEM((TM, D), f32),
            pltpu.VMEM((TM, D), f32),
            pltpu.VMEM((TM, D), f32),
            pltpu.VMEM((TM, D), f32),
            pltpu.VMEM((TM, D), f32),
            pltpu.VMEM((TM, D), f32),
            pltpu.VMEM((TM, D), f32),
            pltpu.VMEM((TM, D), f32),
            pltpu.VMEM((NG, TM, GW), f32),
            pltpu.VMEM((NG, CHUNK, NB * GW), bf16),
            pltpu.VMEM((TM + TAIL, D), f32),
            pltpu.VMEM((NB, D), f32),
            pltpu.VMEM((TM, D), bf16),
            pltpu.VMEM((TM, D), bf16),
            pltpu.VMEM((TM, D), bf16),
        ],
        compiler_params=pltpu.CompilerParams(
            dimension_semantics=("arbitrary",), vmem_limit_bytes=VMEM_LIMIT),
        name="prompt_layer_final" if final else "prompt_layer",
    )(x_tb, x_tb, mod, ng, *([win] * (D_IN // D)), *stacked, fg)


def _sample_kernel(x_ref, mod_ref, h0_ref, cbuf_ref, ng_ref, wu_ref, wv_ref, wga_in_ref,
                   wxb_ref, wgb_ref, wza_ref, wzb_ref, vng_ref, ws0_ref,
                   bs0_ref, cw_ref, cb_ref, wga_ref, wgx_ref, brg_ref, lam_ref,
                   wpa_ref, wpb_ref, wout_ref, fg_ref,
                   y_ref, hs_ref, cs_ref, vs_ref, xs_s):
    wins = (wu_ref, wv_ref, wga_in_ref, wxb_ref, wgb_ref, wza_ref, wzb_ref)
    l = pl.program_id(0)

    @pl.when(l == 0)
    def _():
        xs_s[...] = x_ref[...]

    x = xs_s[...]
    shift = mod_ref[0, :, 0:D]
    scale = mod_ref[0, :, D:2 * D]
    gate = mod_ref[0, :, 2 * D:3 * D]
    h = ((x * _rms_scale(x) * ng_ref[0]) * (1.0 + scale) + shift).astype(bf16)

    def proj(c0):
        return jnp.dot(h, wins[c0 // D][0], preferred_element_type=f32)

    v = proj(C_V)
    v = v * _rms_scale(v) * vng_ref[0]
    vs_ref[0] = v
    s = ws0_ref[0] * v + bs0_ref[0]
    ya = (proj(C_U) * s * _silu(proj(C_GA))).astype(bf16)

    xb = proj(C_XB)
    xc = (cb_ref[0] + cw_ref[0, 0:1, :] * cbuf_ref[0, 0] + cw_ref[0, 1:2, :] * cbuf_ref[0, 1]
          + cw_ref[0, 2:3, :] * cbuf_ref[0, 2] + cw_ref[0, 3:4, :] * xb)
    cs_ref[0, 0] = cbuf_ref[0, 1]
    cs_ref[0, 1] = cbuf_ref[0, 2]
    cs_ref[0, 2] = xb
    xcb = xc.astype(bf16)
    rp = jnp.concatenate(
        [jnp.dot(xcb[:, q * GBLK:(q + 1) * GBLK], wga_ref[0, q], preferred_element_type=f32)
         for q in range(NGB)], axis=1)
    ip = jnp.concatenate(
        [jnp.dot(xcb[:, q * GBLK:(q + 1) * GBLK], wgx_ref[0, q], preferred_element_type=f32)
         for q in range(NGB)], axis=1)
    r = _sigmoid(rp + brg_ref[0, 0:1, :])
    ig = _sigmoid(ip + brg_ref[0, 1:2, :])
    a = jnp.exp(r * _neg_c_softplus(lam_ref[0]))
    hn = a * h0_ref[0] + jnp.sqrt(1.0 - a * a) * (ig * xc)
    hs_ref[0] = hn
    yb = (hn * _silu(proj(C_GB))).astype(bf16)

    pa = jnp.dot(ya, wpa_ref[0], preferred_element_type=f32)
    pb = jnp.dot(yb, wpb_ref[0], preferred_element_type=f32)
    mg = (_sigmoid(proj(C_ZA)) * pa + _sigmoid(proj(C_ZB)) * pb).astype(bf16)
    xn = x + gate * jnp.dot(mg, wout_ref[0], preferred_element_type=f32)
    xs_s[...] = xn

    @pl.when(l == DEPTH - 1)
    def _():
        y_ref[...] = xn * _rms_scale(xn) * fg_ref[...]


def _layer_spec(shape):
    nd = len(shape)
    return pl.BlockSpec((1,) + tuple(shape[1:]), lambda l: (l,) + (0,) * (nd - 1))


def _sample_group(x_s, mod_s, h0, cbuf, ng, win, vng, ws0, bs0, cw, cb, wga, wgx, brg, lam,
                  wpa, wpb, wout, fg):
    head = (h0, cbuf, ng)
    rest = (vng, ws0, bs0, cw, cb, wga, wgx, brg, lam, wpa, wpb, wout)
    win_specs = [pl.BlockSpec((1, D, D), lambda l, c=c: (l, 0, c))
                 for c in range(D_IN // D)]
    return pl.pallas_call(
        _sample_kernel,
        grid=(DEPTH,),
        in_specs=[pl.BlockSpec((NS, D), lambda l: (0, 0)),
                  pl.BlockSpec((1, NS, 3 * D), lambda l: (l, 0, 0))]
        + [_layer_spec(a.shape) for a in head] + win_specs
        + [_layer_spec(a.shape) for a in rest]
        + [pl.BlockSpec((1, D), lambda l: (0, 0))],
        out_specs=[
            pl.BlockSpec((NS, D), lambda l: (0, 0)),
            pl.BlockSpec((1, NS, D), lambda l: (l, 0, 0)),
            pl.BlockSpec((1, CONV_W - 1, NS, D), lambda l: (l, 0, 0, 0)),
            pl.BlockSpec((1, NS, D), lambda l: (l, 0, 0)),
        ],
        out_shape=[
            jax.ShapeDtypeStruct((NS, D), f32),
            jax.ShapeDtypeStruct((DEPTH, NS, D), f32),
            jax.ShapeDtypeStruct((DEPTH, CONV_W - 1, NS, D), f32),
            jax.ShapeDtypeStruct((DEPTH, NS, D), f32),
        ],
        scratch_shapes=[pltpu.VMEM((NS, D), f32)],
        compiler_params=pltpu.CompilerParams(
            dimension_semantics=("arbitrary",), vmem_limit_bytes=VMEM_LIMIT),
        name="sample_layers",
    )(x_s, mod_s, *head, *([win] * (D_IN // D)), *rest, fg)


def _block_diag_gate(w):
    per = GBLK // BW
    w = w.reshape(DEPTH, NGB, per, BW, BW)
    eye = jnp.eye(per, dtype=w.dtype)
    out = jnp.einsum('lqhij,hk->lqhikj', w, eye)
    return out.reshape(DEPTH, NGB, GBLK, GBLK)


def kernel(x_prompt, x_sample, c_prompt, c_sample, state_rglru_h, state_conv, w_ada, b_ada,
           norm_g, w_in, v_norm_g, w_s, b_s, conv_w, conv_b, w_rg_a, b_rg_a, w_rg_x,
           b_rg_x, lam, w_pa, w_pb, w_out, final_g):
    c_all = jnp.concatenate([c_sample, c_prompt, jnp.zeros((8, D), f32)], axis=0)
    mod = _modulation(c_all, w_ada, b_ada)

    win = w_in.astype(bf16)
    wpa = w_pa.astype(bf16)
    wpb = w_pb.astype(bf16)
    wout = w_out.astype(bf16)
    wga = _block_diag_gate(w_rg_a).astype(bf16)
    wgx = _block_diag_gate(w_rg_x).astype(bf16)
    ws = w_s.astype(bf16)
    bs = jnp.broadcast_to(b_s[..., None], (DEPTH, NG, CHUNK, GW))
    ng = norm_g.reshape(DEPTH, 1, D)
    vng = v_norm_g.reshape(DEPTH, 1, D)
    cb = conv_b.reshape(DEPTH, 1, D)
    brg = jnp.stack([b_rg_a, b_rg_x], axis=1)
    lam3 = lam.reshape(DEPTH, 1, D)
    fg = final_g.reshape(1, D)

    x_tb = x_prompt.transpose(1, 0, 2)
    hp, cp = [], []
    for l in range(DEPTH):
        x_tb, h_l, c_l = _prompt_layer(
            l, x_tb, mod, ng, win, vng, ws, bs, conv_w, cb, wga, wgx, brg, lam3,
            wpa, wpb, wout, fg)
        hp.append(h_l)
        cp.append(c_l.reshape(CONV_W - 1, NB, D).transpose(1, 0, 2))
    y_prompt = x_tb.transpose(1, 0, 2)
    h_prompt = jnp.stack(hp)
    conv_prompt = jnp.stack(cp)

    ws0 = jnp.repeat(w_s[:, :, 0, 0], GW, axis=-1).reshape(DEPTH, 1, D)
    bs0 = jnp.repeat(b_s[:, :, 0], GW, axis=-1).reshape(DEPTH, 1, D)
    cbuf = state_conv.transpose(0, 2, 1, 3)
    y_s, h_sample, cs, vs = _sample_group(
        x_sample.reshape(NS, D), mod, state_rglru_h, cbuf, ng, win, vng, ws0, bs0,
        conv_w, cb, wga, wgx, brg, lam3, wpa, wpb, wout, fg)
    y_sample = y_s.reshape(NS, 1, D)
    conv_sample = cs.transpose(0, 2, 1, 3)
    chunk_v_sample = vs.reshape(DEPTH, NS, 1, D)
    return (y_prompt, y_sample, h_prompt, conv_prompt, h_sample, conv_sample,
            chunk_v_sample)
```

```python
import functools

import jax
import jax.numpy as jnp
from jax import lax
from jax.experimental import pallas as pl
from jax.experimental.pallas import tpu as pltpu

D = 1024
NB = 8
SEQ = 2048
DEPTH = 4
NS = 128
CHUNK = 128
GW = 128
NG = D // GW
HB = 16
BW = D // HB
CONV_W = 4
C_RG = 8.0
EPS = 1e-6
D_IN = 7 * D
C_U, C_V, C_GA, C_XB, C_GB, C_ZA, C_ZB = (i * D for i in range(7))

TT = 32
TM = TT * NB
NT = SEQ // TT
RB = 64
NRB = TM // RB
GBLK = 256
NGB = D // GBLK
TAIL = (CONV_W - 1) * NB

VMEM_LIMIT = 62 * 1024 * 1024

f32 = jnp.float32
bf16 = jnp.bfloat16


def _sigmoid(x):
    return jax.nn.sigmoid(x)


def _silu(x):
    return x * jax.nn.sigmoid(x)


def _neg_c_softplus(lam):
    y = -lam
    sp = jnp.maximum(y, 0.0) + jnp.log1p(jnp.exp(-jnp.abs(y)))
    return -C_RG * sp


def _rms_scale(x):
    return lax.rsqrt(jnp.mean(x * x, axis=-1, keepdims=True) + EPS)


def _mod_kernel(c_ref, w_ref, b_ref, o_ref):
    a = _silu(c_ref[...]).astype(bf16)
    w = w_ref[0].astype(bf16)
    o_ref[0] = jnp.dot(a, w, preferred_element_type=f32) + b_ref[0]


def _modulation(c_all, w_ada, b_ada):
    m = c_all.shape[0]
    return pl.pallas_call(
        _mod_kernel,
        grid=(DEPTH, 3),
        in_specs=[
            pl.BlockSpec((m, D), lambda l, j: (0, 0)),
            pl.BlockSpec((1, D, D), lambda l, j: (l, 0, j)),
            pl.BlockSpec((1, 1, D), lambda l, j: (l, 0, j)),
        ],
        out_specs=pl.BlockSpec((1, m, D), lambda l, j: (l, 0, j)),
        out_shape=jax.ShapeDtypeStruct((DEPTH, m, 3 * D), f32),
        compiler_params=pltpu.CompilerParams(
            dimension_semantics=("arbitrary", "arbitrary")),
        name="adaln_mod",
    )(c_all, w_ada, b_ada.reshape(DEPTH, 1, 3 * D))


def _prompt_layer_kernel(final, x_ref, xn_ref, mod_ref, ng_ref, win_ref, vng_ref, ws_ref,
                         bs_ref, cw_ref, cb_ref, wga_ref, wgx_ref, brg_ref, lam_ref,
                         wpa_ref, wpb_ref, wout_ref, fg_ref,
                         xs_ref, mods_ref, h0_ref, cbuf_ref, ws0_ref, bs0_ref,
                         out_ref, hl_ref, cn_ref, xso_ref, hs_ref, cs_ref, vs_ref,
                         h_s, bv, bu, bga, bgb, bxc, brp, bip, bpa, slab, vprime, xb_s,
                         hst, ya_s, yb_s, xcb_s):
    i = pl.program_id(0)
    part = i % (CHUNK // TT)
    hrow = pl.multiple_of(part * TT, TT)
    slot = i % 2

    shift = mod_ref[:, 0:D]
    scale1 = 1.0 + mod_ref[:, D:2 * D]
    gate = mod_ref[:, 2 * D:3 * D]
    ng = ng_ref[...]

    def rows_of(j):
        return slice(j * RB, (j + 1) * RB)

    def prenorm(src_ref, dst_slot):
        def unit(j):
            def run():
                xv = src_ref[j * 8:(j + 1) * 8]
                hv = (xv * _rms_scale(xv) * ng) * scale1[None] + shift[None]
                h_s[dst_slot, rows_of(j), :] = hv.reshape(RB, D).astype(bf16)
            return run
        return [unit(j) for j in range(NRB)]

    @pl.when(i == 0)
    def _():
        hst[...] = jnp.zeros_like(hst)
        xb_s[0:TAIL, :] = jnp.zeros((TAIL, D), f32)
        for f in prenorm(x_ref, 0):
            f()

    @pl.when(part == 0)
    def _():
        vprime[:, TT:CHUNK, :] = jnp.zeros((NG, CHUNK - TT, NB * GW), bf16)

    def mm(dst, lhs, w_ref, c0=0, r0=0):
        def chunk(n):
            def run():
                a = h_s[slot] if lhs is None else lhs[...]
                dst[r0:r0 + TM, n * GBLK:(n + 1) * GBLK] = jnp.dot(
                    a, w_ref[:, c0 + n * GBLK:c0 + (n + 1) * GBLK],
                    preferred_element_type=f32)
            return run
        return [chunk(n) for n in range(NGB)]

    def proj(dst, c0, r0=0):
        return mm(dst, None, win_ref, c0, r0)

    def interleave(mxu, vpu):
        n, m = len(mxu), len(vpu)
        done = 0
        for k, f in enumerate(mxu):
            f()
            upto = (m * (k + 1)) // n
            for g in vpu[done:upto]:
                g()
            done = upto
        for g in vpu[done:]:
            g()

    cb = cb_ref[...]
    cw = [cw_ref[k:k + 1, :] for k in range(CONV_W)]

    def p4a(j):
        def run():
            xc = cb
            for k in range(CONV_W):
                xc = xc + cw[k] * xb_s[j * RB + k * NB:(j + 1) * RB + k * NB, :]
            bxc[rows_of(j), :] = xc
            xcb_s[rows_of(j), :] = xc.astype(bf16)
        return run

    def conv_tail():
        tail = xb_s[TM:TM + TAIL, :]
        cn_ref[...] = tail
        xb_s[0:TAIL, :] = tail

    vng = vng_ref[...]

    def p2(j):
        def run():
            vv = bv[rows_of(j), :]
            vn = vv * _rms_scale(vv) * vng
            for g in range(NG):
                slab[g, rows_of(j), :] = vn[:, g * GW:(g + 1) * GW]
        return run

    def relayout(g):
        def run():
            for b in range(NB):
                vprime[g, pl.ds(hrow, TT), b * GW:(b + 1) * GW] = (
                    slab[g, pl.ds(b, TT, stride=NB), :].astype(bf16))
        return run

    def gates(q):
        def run():
            cols = slice(q * GBLK, (q + 1) * GBLK)
            brp[:, cols] = jnp.dot(xcb_s[:, cols], wga_ref[q], preferred_element_type=f32)
            bip[:, cols] = jnp.dot(xcb_s[:, cols], wgx_ref[q], preferred_element_type=f32)
        return run

    t_idx = hrow + lax.broadcasted_iota(jnp.int32, (TT, CHUNK), 0)
    s_idx = lax.broadcasted_iota(jnp.int32, (TT, CHUNK), 1)
    causal = s_idx <= t_idx

    def spatial(g):
        def run():
            wt = jnp.where(causal, ws_ref[g, pl.ds(hrow, TT), :], jnp.zeros((), bf16))
            sp = jnp.dot(wt, vprime[g], preferred_element_type=f32)
            bias = bs_ref[g, pl.ds(hrow, TT), :]
            for b in range(NB):
                slab[g, pl.ds(b, TT, stride=NB), :] = sp[:, b * GW:(b + 1) * GW] + bias
        return run

    def p3(j):
        def run():
            for g in range(NG):
                cols = slice(g * GW, (g + 1) * GW)
                ya_s[rows_of(j), cols] = (
                    bu[rows_of(j), cols] * slab[g, rows_of(j), :]
                    * _silu(bga[rows_of(j), cols])).astype(bf16)
        return run

    ba = brg_ref[0:1, :]
    bx = brg_ref[1:2, :]
    cneg = _neg_c_softplus(lam_ref[...])
    state = [hst[...]]

    def p4b(j):
        def run():
            xc = bxc[rows_of(j), :]
            r = _sigmoid(brp[rows_of(j), :] + ba)
            ig = _sigmoid(bip[rows_of(j), :] + bx)
            a = jnp.exp(r * cneg)
            xs = jnp.sqrt(1.0 - a * a) * (ig * xc)
            h = state[0]
            ys = []
            for t in range(RB // NB):
                h = a[t * NB:(t + 1) * NB] * h + xs[t * NB:(t + 1) * NB]
                ys.append(h)
            state[0] = h
            yr = jnp.concatenate(ys, axis=0)
            yb_s[rows_of(j), :] = (yr * _silu(bgb[rows_of(j), :])).astype(bf16)
        return run

    def scan_done():
        hst[...] = state[0]
        hl_ref[...] = state[0]

    def p5z(j):
        def run():
            bv[rows_of(j), :] = _sigmoid(bv[rows_of(j), :])
            bpa[rows_of(j), :] = _sigmoid(bpa[rows_of(j), :])
        return run

    def p5a(j):
        def run():
            bu[rows_of(j), :] = bv[rows_of(j), :] * bu[rows_of(j), :]
        return run

    def p5b(q):
        def run():
            cols = slice(q * GBLK, (q + 1) * GBLK)
            for j in range(NRB):
                xcb_s[rows_of(j), cols] = (
                    bu[rows_of(j), cols] + bpa[rows_of(j), cols] * brp[rows_of(j), cols]
                ).astype(bf16)
        return run

    fg = fg_ref[...]

    def p6(j):
        def run():
            o = bgb[rows_of(j), :].reshape(8, NB, D)
            y = x_ref[j * 8:(j + 1) * 8] + gate[None] * o
            if final:
                y = y * _rms_scale(y) * fg
            out_ref[j * 8:(j + 1) * 8] = y
        return run

    def units(f):
        return [f(j) for j in range(NRB)]

    interleave(proj(xb_s, C_XB, TAIL), [])
    interleave(proj(bgb, C_GB), units(p4a) + [conv_tail])
    interleave(proj(bu, C_U), [])
    interleave([gates(q) for q in range(NGB)], [])
    interleave(proj(bv, C_V), units(p4b)[:NRB // 2])
    interleave(proj(bga, C_GA), units(p4b)[NRB // 2:] + [scan_done])
    interleave(proj(bpa, C_ZB), units(p2) + [relayout(g) for g in range(NG)])
    interleave([spatial(g) for g in range(NG)], [])
    interleave(proj(bv, C_ZA), units(p3))
    interleave(mm(bu, ya_s, wpa_ref), units(p5z) + prenorm(xn_ref, 1 - slot))
    interleave(mm(brp, yb_s, wpb_ref), units(p5a))
    interleave([p5b(q) for q in range(NGB)], [])
    interleave(mm(bgb, xcb_s, wout_ref), [])
    for f in units(p6):
        f()

    @pl.when(i == NT - 1)
    def _():
        _sample_layer(final, xs_ref, mods_ref, h0_ref, cbuf_ref, ws0_ref, bs0_ref, ng_ref,
                      win_ref, vng_ref, cw_ref, cb_ref, wga_ref, wgx_ref, brg_ref, lam_ref,
                      wpa_ref, wpb_ref, wout_ref, fg_ref, xso_ref, hs_ref, cs_ref, vs_ref)


def _sample_layer(final, x_ref, mod_ref, h0_ref, cbuf_ref, ws0_ref, bs0_ref, ng_ref, win_ref,
                  vng_ref, cw_ref, cb_ref, wga_ref, wgx_ref, brg_ref, lam_ref, wpa_ref,
                  wpb_ref, wout_ref, fg_ref, xo_ref, hs_ref, cs_ref, vs_ref):
    x = x_ref[...]
    shift = mod_ref[:, 0:D]
    scale = mod_ref[:, D:2 * D]
    gate = mod_ref[:, 2 * D:3 * D]
    h = ((x * _rms_scale(x) * ng_ref[...]) * (1.0 + scale) + shift).astype(bf16)

    def proj(c0):
        return jnp.dot(h, win_ref[:, c0:c0 + D], preferred_element_type=f32)

    v = proj(C_V)
    v = v * _rms_scale(v) * vng_ref[...]
    vs_ref[...] = v
    s = ws0_ref[...] * v + bs0_ref[...]
    ya = (proj(C_U) * s * _silu(proj(C_GA))).astype(bf16)

    xb = proj(C_XB)
    xc = (cb_ref[...] + cw_ref[0:1, :] * cbuf_ref[0] + cw_ref[1:2, :] * cbuf_ref[1]
          + cw_ref[2:3, :] * cbuf_ref[2] + cw_ref[3:4, :] * xb)
    cs_ref[0] = cbuf_ref[1]
    cs_ref[1] = cbuf_ref[2]
    cs_ref[2] = xb
    xcb = xc.astype(bf16)
    rp = jnp.concatenate(
        [jnp.dot(xcb[:, q * GBLK:(q + 1) * GBLK], wga_ref[q], preferred_element_type=f32)
         for q in range(NGB)], axis=1)
    ip = jnp.concatenate(
        [jnp.dot(xcb[:, q * GBLK:(q + 1) * GBLK], wgx_ref[q], preferred_element_type=f32)
         for q in range(NGB)], axis=1)
    r = _sigmoid(rp + brg_ref[0:1, :])
    ig = _sigmoid(ip + brg_ref[1:2, :])
    a = jnp.exp(r * _neg_c_softplus(lam_ref[...]))
    hn = a * h0_ref[...] + jnp.sqrt(1.0 - a * a) * (ig * xc)
    hs_ref[...] = hn
    yb = (hn * _silu(proj(C_GB))).astype(bf16)

    pa = jnp.dot(ya, wpa_ref[...], preferred_element_type=f32)
    pb = jnp.dot(yb, wpb_ref[...], preferred_element_type=f32)
    mg = (_sigmoid(proj(C_ZA)) * pa + _sigmoid(proj(C_ZB)) * pb).astype(bf16)
    xn = x + gate * jnp.dot(mg, wout_ref[...], preferred_element_type=f32)
    if final:
        xn = xn * _rms_scale(xn) * fg_ref[...]
    xo_ref[...] = xn


def _layer_const_spec(shape, l):
    nd = len(shape)
    return pl.BlockSpec((None,) + tuple(shape[1:]), lambda i: (l,) + (0,) * (nd - 1),
                        pipeline_mode=pl.Buffered(1))


def _layer(l, x_tb, x_s, mod, ng, win, vng, ws, bs, cw, cb, wga, wgx, brg, lam,
           wpa, wpb, wout, fg, h0, cbuf, ws0, bs0):
    final = l == DEPTH - 1
    stacked = (ng, win, vng, ws, bs, cw, cb, wga, wgx, brg, lam, wpa, wpb, wout)
    sample_stacked = (h0, cbuf, ws0, bs0)
    mod_spec = pl.BlockSpec((None, NB, 3 * D), lambda i: (l, NS // NB, 0),
                            pipeline_mode=pl.Buffered(1))
    mods_spec = pl.BlockSpec((None, NS, 3 * D), lambda i: (l, 0, 0),
                             pipeline_mode=pl.Buffered(1))
    fg_spec = pl.BlockSpec((1, D), lambda i: (0, 0), pipeline_mode=pl.Buffered(1))
    xs_spec = pl.BlockSpec((NS, D), lambda i: (0, 0), pipeline_mode=pl.Buffered(1))
    return pl.pallas_call(
        functools.partial(_prompt_layer_kernel, final),
        grid=(NT,),
        in_specs=[pl.BlockSpec((TT, NB, D), lambda i: (i, 0, 0)),
                  pl.BlockSpec((TT, NB, D), lambda i: (jnp.minimum(i + 1, NT - 1), 0, 0))]
        + [mod_spec] + [_layer_const_spec(a.shape, l) for a in stacked] + [fg_spec]
        + [xs_spec, mods_spec] + [_layer_const_spec(a.shape, l) for a in sample_stacked],
        out_specs=[
            pl.BlockSpec((TT, NB, D), lambda i: (i, 0, 0)),
            pl.BlockSpec((NB, D), lambda i: (0, 0)),
            pl.BlockSpec((TAIL, D), lambda i: (0, 0)),
            pl.BlockSpec((NS, D), lambda i: (0, 0)),
            pl.BlockSpec((NS, D), lambda i: (0, 0)),
            pl.BlockSpec((CONV_W - 1, NS, D), lambda i: (0, 0, 0)),
            pl.BlockSpec((NS, D), lambda i: (0, 0)),
        ],
        out_shape=[
            jax.ShapeDtypeStruct((SEQ, NB, D), f32),
            jax.ShapeDtypeStruct((NB, D), f32),
            jax.ShapeDtypeStruct((TAIL, D), f32),
            jax.ShapeDtypeStruct((NS, D), f32),
            jax.ShapeDtypeStruct((NS, D), f32),
            jax.ShapeDtypeStruct((CONV_W - 1, NS, D), f32),
            jax.ShapeDtypeStruct((NS, D), f32),
        ],
        scratch_shapes=[
            pltpu.VMEM((2, TM, D), bf16),
            pltpu.VMEM((TM, D), f32),
            pltpu.VMEM((TM, D), f32),
            pltpu.VMEM((TM, D), f32),
            pltpu.VMEM((TM, D), f32),
            pltpu.VMEM((TM, D), f32),
            pltpu.VMEM((TM, D), f32),
            pltpu.VMEM((TM, D), f32),
            pltpu.VMEM((TM, D), f32),
            pltpu.VMEM((NG, TM, GW), f32),
            pltpu.VMEM((NG, CHUNK, NB * GW), bf16),
            pltpu.VMEM((TM + TAIL, D), f32),
            pltpu.VMEM((NB, D), f32),
            pltpu.VMEM((TM, D), bf16),
            pltpu.VMEM((TM, D), bf16),
            pltpu.VMEM((TM, D), bf16),
        ],
        compiler_params=pltpu.CompilerParams(
            dimension_semantics=("arbitrary",), vmem_limit_bytes=VMEM_LIMIT),
        name="layer_final" if final else "layer",
    )(x_tb, x_tb, mod, *stacked, fg, x_s, mod, *sample_stacked)


def _block_diag_gate(w):
    per = GBLK // BW
    w = w.reshape(DEPTH, NGB, per, BW, BW)
    eye = jnp.eye(per, dtype=w.dtype)
    out = jnp.einsum('lqhij,hk->lqhikj', w, eye)
    return out.reshape(DEPTH, NGB, GBLK, GBLK)


def kernel(x_prompt, x_sample, c_prompt, c_sample, state_rglru_h, state_conv, w_ada, b_ada,
           norm_g, w_in, v_norm_g, w_s, b_s, conv_w, conv_b, w_rg_a, b_rg_a, w_rg_x,
           b_rg_x, lam, w_pa, w_pb, w_out, final_g):
    c_all = jnp.concatenate([c_sample, c_prompt, jnp.zeros((8, D), f32)], axis=0)
    mod = _modulation(c_all, w_ada, b_ada)

    win = w_in.astype(bf16)
    wpa = w_pa.astype(bf16)
    wpb = w_pb.astype(bf16)
    wout = w_out.astype(bf16)
    wga = _block_diag_gate(w_rg_a).astype(bf16)
    wgx = _block_diag_gate(w_rg_x).astype(bf16)
    ws = w_s.astype(bf16)
    bs = jnp.broadcast_to(b_s[..., None], (DEPTH, NG, CHUNK, GW))
    ng = norm_g.reshape(DEPTH, 1, D)
    vng = v_norm_g.reshape(DEPTH, 1, D)
    cb = conv_b.reshape(DEPTH, 1, D)
    brg = jnp.stack([b_rg_a, b_rg_x], axis=1)
    lam3 = lam.reshape(DEPTH, 1, D)
    fg = final_g.reshape(1, D)

    ws0 = jnp.repeat(w_s[:, :, 0, 0], GW, axis=-1).reshape(DEPTH, 1, D)
    bs0 = jnp.repeat(b_s[:, :, 0], GW, axis=-1).reshape(DEPTH, 1, D)
    cbuf = state_conv.transpose(0, 2, 1, 3)

    x_tb = x_prompt.transpose(1, 0, 2)
    x_s = x_sample.reshape(NS, D)
    hp, cp, hs, cs, vs = [], [], [], [], []
    for l in range(DEPTH):
        x_tb, h_l, c_l, x_s, hs_l, cs_l, vs_l = _layer(
            l, x_tb, x_s, mod, ng, win, vng, ws, bs, conv_w, cb, wga, wgx, brg, lam3,
            wpa, wpb, wout, fg, state_rglru_h, cbuf, ws0, bs0)
        hp.append(h_l)
        cp.append(c_l.reshape(CONV_W - 1, NB, D).transpose(1, 0, 2))
        hs.append(hs_l)
        cs.append(cs_l.transpose(1, 0, 2))
        vs.append(vs_l)
    y_prompt = x_tb.transpose(1, 0, 2)
    h_prompt = jnp.stack(hp)
    conv_prompt = jnp.stack(cp)
    y_sample = x_s.reshape(NS, 1, D)
    h_sample = jnp.stack(hs)
    conv_sample = jnp.stack(cs)
    chunk_v_sample = jnp.stack(vs).reshape(DEPTH, NS, 1, D)
    return (y_prompt, y_sample, h_prompt, conv_prompt, h_sample, conv_sample,
            chunk_v_sample)
```

```python
import functools

import jax
import jax.numpy as jnp
from jax import lax
from jax.experimental import pallas as pl
from jax.experimental.pallas import tpu as pltpu

D = 1024
NB = 8
SEQ = 2048
DEPTH = 4
NS = 128
CHUNK = 128
GW = 128
NG = D // GW
HB = 16
BW = D // HB
CONV_W = 4
C_RG = 8.0
EPS = 1e-6
D_IN = 7 * D
C_U, C_V, C_GA, C_XB, C_GB, C_ZA, C_ZB = (i * D for i in range(7))

TT = 32
TM = TT * NB
NT = SEQ // TT
RB = 64
NRB = TM // RB
GBLK = 256
NGB = D // GBLK
TAIL = (CONV_W - 1) * NB

VMEM_LIMIT = 62 * 1024 * 1024

f32 = jnp.float32
bf16 = jnp.bfloat16


def _sigmoid(x):
    return 0.5 * jnp.tanh(0.5 * x) + 0.5


def _silu(x):
    return x * _sigmoid(x)


def _sqrt_unit_interval(x):
    return jnp.where(x == 0.0, 0.0, x * lax.rsqrt(x))


LOG2E = 1.4426950408889634


def _log2_decay(lam):
    y = -lam
    sp = jnp.maximum(y, 0.0) + jnp.log1p(jnp.exp(-jnp.abs(y)))
    return (-C_RG * LOG2E) * sp


def _rms_scale(x):
    return lax.rsqrt(jnp.mean(x * x, axis=-1, keepdims=True) + EPS)


def _pack_rows(w):
    w = w.astype(bf16)
    *lead, k, n = w.shape
    pairs = jnp.swapaxes(w.reshape(*lead, k // 2, 2, n), -1, -2)
    return lax.bitcast_convert_type(pairs, jnp.uint32)


def _unpack(w_u32):
    return pltpu.bitcast(w_u32, bf16)


def _mod_kernel(c_ref, w_ref, b_ref, o_ref):
    a = _silu(c_ref[...]).astype(bf16)
    w = w_ref[0].astype(bf16)
    o_ref[0] = jnp.dot(a, w, preferred_element_type=f32) + b_ref[0]


def _modulation(c_all, w_ada, b_ada):
    m = c_all.shape[0]
    return pl.pallas_call(
        _mod_kernel,
        grid=(DEPTH, 3),
        in_specs=[
            pl.BlockSpec((m, D), lambda l, j: (0, 0)),
            pl.BlockSpec((1, D, D), lambda l, j: (l, 0, j)),
            pl.BlockSpec((1, 1, D), lambda l, j: (l, 0, j)),
        ],
        out_specs=pl.BlockSpec((1, m, D), lambda l, j: (l, 0, j)),
        out_shape=jax.ShapeDtypeStruct((DEPTH, m, 3 * D), f32),
        compiler_params=pltpu.CompilerParams(
            dimension_semantics=("arbitrary", "arbitrary")),
        name="adaln_mod",
    )(c_all, w_ada, b_ada.reshape(DEPTH, 1, 3 * D))


def _prompt_layer_kernel(final, x_ref, xn_ref, mod_ref, ng_ref, win_ref, vng_ref, ws_ref,
                         bs_ref, cw_ref, cb_ref, wga_ref, wgx_ref, brg_ref, lam_ref,
                         wpa_ref, wpb_ref, wout_ref, fg_ref,
                         out_ref, hl_ref, cn_ref,
                         h_s, bv, bu, bga, bgb, bxc, brp, bip, bpa, slab, vprime, xb_s,
                         hst, ya_s, yb_s, xcb_s):
    i = pl.program_id(0)
    part = i % (CHUNK // TT)
    hrow = pl.multiple_of(part * TT, TT)
    slot = i % 2

    shift = mod_ref[:, 0:D]
    scale1 = 1.0 + mod_ref[:, D:2 * D]
    gate = mod_ref[:, 2 * D:3 * D]
    ng = ng_ref[...]

    def rows_of(j):
        return slice(j * RB, (j + 1) * RB)

    def prenorm(src_ref, dst_slot):
        def unit(j):
            def run():
                xv = src_ref[j * 8:(j + 1) * 8]
                hv = (xv * _rms_scale(xv) * ng) * scale1[None] + shift[None]
                h_s[dst_slot, rows_of(j), :] = hv.reshape(RB, D).astype(bf16)
            return run
        return [unit(j) for j in range(NRB)]

    @pl.when(i == 0)
    def _():
        hst[...] = jnp.zeros_like(hst)
        xb_s[0:TAIL, :] = jnp.zeros((TAIL, D), f32)
        for f in prenorm(x_ref, 0):
            f()

    @pl.when(part == 0)
    def _():
        vprime[:, TT:CHUNK, :] = jnp.zeros((NG, CHUNK - TT, NB * GW), bf16)

    def mm(dst, lhs, w_ref, c0=0, r0=0):
        def chunk(n):
            def run():
                a = h_s[slot] if lhs is None else lhs[...]
                dst[r0:r0 + TM, n * GBLK:(n + 1) * GBLK] = jnp.dot(
                    a, _unpack(w_ref[:, c0 + n * GBLK:c0 + (n + 1) * GBLK]),
                    preferred_element_type=f32)
            return run
        return [chunk(n) for n in range(NGB)]

    def proj(dst, c0, r0=0):
        return mm(dst, None, win_ref, c0, r0)

    def interleave(mxu, vpu):
        n, m = len(mxu), len(vpu)
        done = 0
        for k, f in enumerate(mxu):
            upto = (m * (k + 1)) // n
            for g in vpu[done:upto]:
                g()
            done = upto
            f()

    cb = cb_ref[...]
    cw = [cw_ref[k:k + 1, :] for k in range(CONV_W)]

    def p4a(j):
        def run():
            xc = cb
            for k in range(CONV_W):
                xc = xc + cw[k] * xb_s[j * RB + k * NB:(j + 1) * RB + k * NB, :]
            bxc[rows_of(j), :] = xc
            xcb_s[rows_of(j), :] = xc.astype(bf16)
        return run

    def conv_tail():
        tail = xb_s[TM:TM + TAIL, :]
        cn_ref[...] = tail
        xb_s[0:TAIL, :] = tail

    vng = vng_ref[...]

    def p2(j):
        def run():
            vv = bv[rows_of(j), :]
            vn = vv * _rms_scale(vv) * vng
            for g in range(NG):
                slab[g, rows_of(j), :] = vn[:, g * GW:(g + 1) * GW]
        return run

    def relayout(g):
        def run():
            for b in range(NB):
                vprime[g, pl.ds(hrow, TT), b * GW:(b + 1) * GW] = (
                    slab[g, pl.ds(b, TT, stride=NB), :].astype(bf16))
        return run

    def gates(q):
        def run():
            cols = slice(q * GBLK, (q + 1) * GBLK)
            brp[:, cols] = jnp.dot(xcb_s[:, cols], _unpack(wga_ref[q]),
                                   preferred_element_type=f32)
            bip[:, cols] = jnp.dot(xcb_s[:, cols], _unpack(wgx_ref[q]),
                                   preferred_element_type=f32)
        return run

    t_idx = hrow + lax.broadcasted_iota(jnp.int32, (TT, CHUNK), 0)
    s_idx = lax.broadcasted_iota(jnp.int32, (TT, CHUNK), 1)
    causal = s_idx <= t_idx

    def spatial(g):
        def run():
            wt = jnp.where(causal, ws_ref[g, pl.ds(hrow, TT), :], jnp.zeros((), bf16))
            sp = jnp.dot(wt, vprime[g], preferred_element_type=f32)
            bias = bs_ref[g, pl.ds(hrow, TT), :]
            for b in range(NB):
                slab[g, pl.ds(b, TT, stride=NB), :] = sp[:, b * GW:(b + 1) * GW] + bias
        return run

    def p3(j):
        def run():
            for g in range(NG):
                cols = slice(g * GW, (g + 1) * GW)
                ya_s[rows_of(j), cols] = (
                    bu[rows_of(j), cols] * slab[g, rows_of(j), :]
                    * _silu(bga[rows_of(j), cols])).astype(bf16)
        return run

    ba = brg_ref[0:1, :]
    bx = brg_ref[1:2, :]
    cneg = _log2_decay(lam_ref[...])
    state = [hst[...]]

    def p4b(j):
        def run():
            xc = bxc[rows_of(j), :]
            r = _sigmoid(brp[rows_of(j), :] + ba)
            ig = _sigmoid(bip[rows_of(j), :] + bx)
            a = jnp.exp2(r * cneg)
            xs = _sqrt_unit_interval(1.0 - a * a) * (ig * xc)
            h = state[0]
            ys = []
            for t in range(RB // NB):
                h = a[t * NB:(t + 1) * NB] * h + xs[t * NB:(t + 1) * NB]
                ys.append(h)
            state[0] = h
            yr = jnp.concatenate(ys, axis=0)
            yb_s[rows_of(j), :] = (yr * bgb[rows_of(j), :]).astype(bf16)
        return run

    def gb_act(q):
        def run():
            cols = slice(q * GBLK, (q + 1) * GBLK)
            bgb[:, cols] = _silu(bgb[:, cols])
        return run

    def scan_done():
        hst[...] = state[0]
        hl_ref[...] = state[0]

    def p5z(j):
        def run():
            bv[rows_of(j), :] = _sigmoid(bv[rows_of(j), :])
            bpa[rows_of(j), :] = _sigmoid(bpa[rows_of(j), :])
        return run

    def p5a(j):
        def run():
            bu[rows_of(j), :] = bv[rows_of(j), :] * bu[rows_of(j), :]
        return run

    def p5b(q):
        def run():
            cols = slice(q * GBLK, (q + 1) * GBLK)
            for j in range(NRB):
                xcb_s[rows_of(j), cols] = (
                    bu[rows_of(j), cols] + bpa[rows_of(j), cols] * brp[rows_of(j), cols]
                ).astype(bf16)
        return run

    fg = fg_ref[...]

    def p6(j):
        def run():
            o = bgb[rows_of(j), :].reshape(8, NB, D)
            y = x_ref[j * 8:(j + 1) * 8] + gate[None] * o
            if final:
                y = y * _rms_scale(y) * fg
            out_ref[j * 8:(j + 1) * 8] = y
        return run

    def units(f):
        return [f(j) for j in range(NRB)]

    interleave(proj(xb_s, C_XB, TAIL), [])
    interleave(proj(bgb, C_GB), units(p4a) + [conv_tail])
    interleave(proj(bu, C_U), [gb_act(q) for q in range(NGB)])
    interleave([gates(q) for q in range(NGB)], [])
    interleave(proj(bv, C_V), units(p4b)[:NRB // 2])
    interleave(proj(bga, C_GA), units(p4b)[NRB // 2:] + [scan_done])
    interleave(proj(bpa, C_ZB), units(p2) + [relayout(g) for g in range(NG)])
    interleave([spatial(g) for g in range(NG)], [])
    interleave(proj(bv, C_ZA), units(p3))
    interleave(mm(bu, ya_s, wpa_ref), units(p5z) + prenorm(xn_ref, 1 - slot))
    interleave(mm(brp, yb_s, wpb_ref), units(p5a))
    interleave([p5b(q) for q in range(NGB)], [])
    interleave(mm(bgb, xcb_s, wout_ref), [])
    for f in units(p6):
        f()


def _layer_const_spec(shape, l):
    nd = len(shape)
    return pl.BlockSpec((None,) + tuple(shape[1:]), lambda i: (l,) + (0,) * (nd - 1),
                        pipeline_mode=pl.Buffered(1))


def _prompt_layer(l, x_tb, mod, ng, win, vng, ws, bs, cw, cb, wga, wgx, brg, lam,
                  wpa, wpb, wout, fg):
    final = l == DEPTH - 1
    stacked = (ng, win, vng, ws, bs, cw, cb, wga, wgx, brg, lam, wpa, wpb, wout)
    mod_spec = pl.BlockSpec((None, NB, 3 * D), lambda i: (l, NS // NB, 0),
                            pipeline_mode=pl.Buffered(1))
    fg_spec = pl.BlockSpec((1, D), lambda i: (0, 0), pipeline_mode=pl.Buffered(1))
    return pl.pallas_call(
        functools.partial(_prompt_layer_kernel, final),
        grid=(NT,),
        in_specs=[pl.BlockSpec((TT, NB, D), lambda i: (i, 0, 0)),
                  pl.BlockSpec((TT, NB, D), lambda i: (jnp.minimum(i + 1, NT - 1), 0, 0))]
        + [mod_spec] + [_layer_const_spec(a.shape, l) for a in stacked] + [fg_spec],
        out_specs=[
            pl.BlockSpec((TT, NB, D), lambda i: (i, 0, 0)),
            pl.BlockSpec((NB, D), lambda i: (0, 0)),
            pl.BlockSpec((TAIL, D), lambda i: (0, 0)),
        ],
        out_shape=[
            jax.ShapeDtypeStruct((SEQ, NB, D), f32),
            jax.ShapeDtypeStruct((NB, D), f32),
            jax.ShapeDtypeStruct((TAIL, D), f32),
        ],
        scratch_shapes=[
            pltpu.VMEM((2, TM, D), bf16),
            pltpu.VMEM((TM, D), f32),
            pltpu.VMEM((TM, D), f32),
            pltpu.VMEM((TM, D), f32),
            pltpu.VMEM((TM, D), f32),
            pltpu.VMEM((TM, D), f32),
            pltpu.VMEM((TM, D), f32),
            pltpu.VMEM((TM, D), f32),
            pltpu.VMEM((TM, D), f32),
            pltpu.VMEM((NG, TM, GW), f32),
            pltpu.VMEM((NG, CHUNK, NB * GW), bf16),
            pltpu.VMEM((TM + TAIL, D), f32),
            pltpu.VMEM((NB, D), f32),
            pltpu.VMEM((TM, D), bf16),
            pltpu.VMEM((TM, D), bf16),
            pltpu.VMEM((TM, D), bf16),
        ],
        compiler_params=pltpu.CompilerParams(
            dimension_semantics=("arbitrary",), vmem_limit_bytes=VMEM_LIMIT),
        name="prompt_layer_final" if final else "prompt_layer",
    )(x_tb, x_tb, mod, *stacked, fg)


def _sample_kernel(x_ref, mod_ref, h0_ref, cbuf_ref, ng_ref, win_ref, vng_ref, ws0_ref,
                   bs0_ref, cw_ref, cb_ref, wga_ref, wgx_ref, brg_ref, lam_ref,
                   wpa_ref, wpb_ref, wout_ref, fg_ref,
                   y_ref, hs_ref, cs_ref, vs_ref, xs_s):
    l = pl.program_id(0)

    @pl.when(l == 0)
    def _():
        xs_s[...] = x_ref[...]

    x = xs_s[...]
    shift = mod_ref[0, :, 0:D]
    scale = mod_ref[0, :, D:2 * D]
    gate = mod_ref[0, :, 2 * D:3 * D]
    h = ((x * _rms_scale(x) * ng_ref[0]) * (1.0 + scale) + shift).astype(bf16)

    def proj(c0):
        return jnp.dot(h, _unpack(win_ref[0, :, c0:c0 + D]), preferred_element_type=f32)

    v = proj(C_V)
    v = v * _rms_scale(v) * vng_ref[0]
    vs_ref[0] = v
    s = ws0_ref[0] * v + bs0_ref[0]
    ya = (proj(C_U) * s * _silu(proj(C_GA))).astype(bf16)

    xb = proj(C_XB)
    xc = (cb_ref[0] + cw_ref[0, 0:1, :] * cbuf_ref[0, 0] + cw_ref[0, 1:2, :] * cbuf_ref[0, 1]
          + cw_ref[0, 2:3, :] * cbuf_ref[0, 2] + cw_ref[0, 3:4, :] * xb)
    cs_ref[0, 0] = cbuf_ref[0, 1]
    cs_ref[0, 1] = cbuf_ref[0, 2]
    cs_ref[0, 2] = xb
    xcb = xc.astype(bf16)
    rp = jnp.concatenate(
        [jnp.dot(xcb[:, q * GBLK:(q + 1) * GBLK], _unpack(wga_ref[0, q]),
                 preferred_element_type=f32) for q in range(NGB)], axis=1)
    ip = jnp.concatenate(
        [jnp.dot(xcb[:, q * GBLK:(q + 1) * GBLK], _unpack(wgx_ref[0, q]),
                 preferred_element_type=f32) for q in range(NGB)], axis=1)
    r = _sigmoid(rp + brg_ref[0, 0:1, :])
    ig = _sigmoid(ip + brg_ref[0, 1:2, :])
    a = jnp.exp2(r * _log2_decay(lam_ref[0]))
    hn = a * h0_ref[0] + _sqrt_unit_interval(1.0 - a * a) * (ig * xc)
    hs_ref[0] = hn
    yb = (hn * _silu(proj(C_GB))).astype(bf16)

    pa = jnp.dot(ya, _unpack(wpa_ref[0]), preferred_element_type=f32)
    pb = jnp.dot(yb, _unpack(wpb_ref[0]), preferred_element_type=f32)
    mg = (_sigmoid(proj(C_ZA)) * pa + _sigmoid(proj(C_ZB)) * pb).astype(bf16)
    xn = x + gate * jnp.dot(mg, _unpack(wout_ref[0]), preferred_element_type=f32)
    xs_s[...] = xn

    @pl.when(l == DEPTH - 1)
    def _():
        y_ref[...] = xn * _rms_scale(xn) * fg_ref[...]


def _layer_spec(shape):
    nd = len(shape)
    return pl.BlockSpec((1,) + tuple(shape[1:]), lambda l: (l,) + (0,) * (nd - 1))


def _sample_group(x_s, mod_s, h0, cbuf, ng, win, vng, ws0, bs0, cw, cb, wga, wgx, brg, lam,
                  wpa, wpb, wout, fg):
    per_layer = (mod_s, h0, cbuf, ng, win, vng, ws0, bs0, cw, cb, wga, wgx, brg, lam,
                 wpa, wpb, wout)
    return pl.pallas_call(
        _sample_kernel,
        grid=(DEPTH,),
        in_specs=[pl.BlockSpec((NS, D), lambda l: (0, 0)),
                  pl.BlockSpec((1, NS, 3 * D), lambda l: (l, 0, 0))]
        + [_layer_spec(a.shape) for a in per_layer[1:]]
        + [pl.BlockSpec((1, D), lambda l: (0, 0))],
        out_specs=[
            pl.BlockSpec((NS, D), lambda l: (0, 0)),
            pl.BlockSpec((1, NS, D), lambda l: (l, 0, 0)),
            pl.BlockSpec((1, CONV_W - 1, NS, D), lambda l: (l, 0, 0, 0)),
            pl.BlockSpec((1, NS, D), lambda l: (l, 0, 0)),
        ],
        out_shape=[
            jax.ShapeDtypeStruct((NS, D), f32),
            jax.ShapeDtypeStruct((DEPTH, NS, D), f32),
            jax.ShapeDtypeStruct((DEPTH, CONV_W - 1, NS, D), f32),
            jax.ShapeDtypeStruct((DEPTH, NS, D), f32),
        ],
        scratch_shapes=[pltpu.VMEM((NS, D), f32)],
        compiler_params=pltpu.CompilerParams(
            dimension_semantics=("arbitrary",), vmem_limit_bytes=VMEM_LIMIT),
        name="sample_layers",
    )(x_s, *per_layer, fg)


def _block_diag_gate(w):
    per = GBLK // BW
    w = w.reshape(DEPTH, NGB, per, BW, BW)
    eye = jnp.eye(per, dtype=w.dtype)
    out = jnp.einsum('lqhij,hk->lqhikj', w, eye)
    return out.reshape(DEPTH, NGB, GBLK, GBLK)


def kernel(x_prompt, x_sample, c_prompt, c_sample, state_rglru_h, state_conv, w_ada, b_ada,
           norm_g, w_in, v_norm_g, w_s, b_s, conv_w, conv_b, w_rg_a, b_rg_a, w_rg_x,
           b_rg_x, lam, w_pa, w_pb, w_out, final_g):
    c_all = jnp.concatenate([c_sample, c_prompt, jnp.zeros((8, D), f32)], axis=0)
    mod = _modulation(c_all, w_ada, b_ada)

    win = _pack_rows(w_in)
    wpa = _pack_rows(w_pa)
    wpb = _pack_rows(w_pb)
    wout = _pack_rows(w_out)
    wga = _pack_rows(_block_diag_gate(w_rg_a))
    wgx = _pack_rows(_block_diag_gate(w_rg_x))
    ws = w_s.astype(bf16)
    bs = jnp.broadcast_to(b_s[..., None], (DEPTH, NG, CHUNK, GW))
    ng = norm_g.reshape(DEPTH, 1, D)
    vng = v_norm_g.reshape(DEPTH, 1, D)
    cb = conv_b.reshape(DEPTH, 1, D)
    brg = jnp.stack([b_rg_a, b_rg_x], axis=1)
    lam3 = lam.reshape(DEPTH, 1, D)
    fg = final_g.reshape(1, D)

    x_tb = x_prompt.transpose(1, 0, 2)
    hp, cp = [], []
    for l in range(DEPTH):
        x_tb, h_l, c_l = _prompt_layer(
            l, x_tb, mod, ng, win, vng, ws, bs, conv_w, cb, wga, wgx, brg, lam3,
            wpa, wpb, wout, fg)
        hp.append(h_l)
        cp.append(c_l.reshape(CONV_W - 1, NB, D).transpose(1, 0, 2))
    y_prompt = x_tb.transpose(1, 0, 2)
    h_prompt = jnp.stack(hp)
    conv_prompt = jnp.stack(cp)

    ws0 = jnp.repeat(w_s[:, :, 0, 0], GW, axis=-1).reshape(DEPTH, 1, D)
    bs0 = jnp.repeat(b_s[:, :, 0], GW, axis=-1).reshape(DEPTH, 1, D)
    cbuf = state_conv.transpose(0, 2, 1, 3)
    y_s, h_sample, cs, vs = _sample_group(
        x_sample.reshape(NS, D), mod, state_rglru_h, cbuf, ng, win, vng, ws0, bs0,
        conv_w, cb, wga, wgx, brg, lam3, wpa, wpb, wout, fg)
    y_sample = y_s.reshape(NS, 1, D)
    conv_sample = cs.transpose(0, 2, 1, 3)
    chunk_v_sample = vs.reshape(DEPTH, NS, 1, D)
    return (y_prompt, y_sample, h_prompt, conv_prompt, h_sample, conv_sample,
            chunk_v_sample)
```

```python
import functools

import jax
import jax.numpy as jnp
from jax import lax
from jax.experimental import pallas as pl
from jax.experimental.pallas import tpu as pltpu

D = 1024
NB = 8
SEQ = 2048
DEPTH = 4
NS = 128
CHUNK = 128
GW = 128
NG = D // GW
HB = 16
BW = D // HB
CONV_W = 4
C_RG = 8.0
EPS = 1e-6
D_IN = 7 * D
C_U, C_V, C_GA, C_XB, C_GB, C_ZA, C_ZB = (i * D for i in range(7))

TT = 32
TM = TT * NB
NT = SEQ // TT
RB = 64
NRB = TM // RB
GBLK = 256
NGB = D // GBLK
TAIL = (CONV_W - 1) * NB

VMEM_LIMIT = 62 * 1024 * 1024

f32 = jnp.float32
bf16 = jnp.bfloat16


def _sigmoid(x):
    return 0.5 * jnp.tanh(0.5 * x) + 0.5


def _silu(x):
    return x * _sigmoid(x)


def _sqrt_unit_interval(x):
    return jnp.where(x == 0.0, 0.0, x * lax.rsqrt(x))


LOG2E = 1.4426950408889634


def _log2_decay(lam):
    y = -lam
    sp = jnp.maximum(y, 0.0) + jnp.log1p(jnp.exp(-jnp.abs(y)))
    return (-C_RG * LOG2E) * sp


def _rms_scale(x):
    return lax.rsqrt(jnp.mean(x * x, axis=-1, keepdims=True) + EPS)


def _mod_kernel(c_ref, w_ref, b_ref, o_ref):
    a = _silu(c_ref[...]).astype(bf16)
    w = w_ref[0].astype(bf16)
    o_ref[0] = jnp.dot(a, w, preferred_element_type=f32) + b_ref[0]


def _modulation(c_all, w_ada, b_ada):
    m = c_all.shape[0]
    return pl.pallas_call(
        _mod_kernel,
        grid=(DEPTH, 3),
        in_specs=[
            pl.BlockSpec((m, D), lambda l, j: (0, 0)),
            pl.BlockSpec((1, D, D), lambda l, j: (l, 0, j)),
            pl.BlockSpec((1, 1, D), lambda l, j: (l, 0, j)),
        ],
        out_specs=pl.BlockSpec((1, m, D), lambda l, j: (l, 0, j)),
        out_shape=jax.ShapeDtypeStruct((DEPTH, m, 3 * D), f32),
        compiler_params=pltpu.CompilerParams(
            dimension_semantics=("arbitrary", "arbitrary")),
        name="adaln_mod",
    )(c_all, w_ada, b_ada.reshape(DEPTH, 1, 3 * D))


def _prompt_layer_kernel(final, x_ref, xn_ref, mod_ref, ng_ref, win_ref, vng_ref, ws_ref,
                         bs_ref, cw_ref, cb_ref, wga_ref, wgx_ref, brg_ref, lam_ref,
                         wpa_ref, wpb_ref, wout_ref, fg_ref,
                         out_ref, hl_ref, cn_ref,
                         h_s, bv, bu, bga, bgb, bxc, brp, bip, bpa, slab, vprime, xb_s,
                         hst, ya_s, yb_s, xcb_s):
    i = pl.program_id(0)
    part = i % (CHUNK // TT)
    hrow = pl.multiple_of(part * TT, TT)
    slot = i % 2

    shift = mod_ref[:, 0:D]
    scale1 = 1.0 + mod_ref[:, D:2 * D]
    gate = mod_ref[:, 2 * D:3 * D]
    ng = ng_ref[...]

    def rows_of(j):
        return slice(j * RB, (j + 1) * RB)

    def prenorm(src_ref, dst_slot):
        def unit(j):
            def run():
                xv = src_ref[j * 8:(j + 1) * 8]
                hv = (xv * _rms_scale(xv) * ng) * scale1[None] + shift[None]
                h_s[dst_slot, rows_of(j), :] = hv.reshape(RB, D).astype(bf16)
            return run
        return [unit(j) for j in range(NRB)]

    @pl.when(i == 0)
    def _():
        hst[...] = jnp.zeros_like(hst)
        xb_s[0:TAIL, :] = jnp.zeros((TAIL, D), f32)
        for f in prenorm(x_ref, 0):
            f()

    @pl.when(part == 0)
    def _():
        vprime[:, TT:CHUNK, :] = jnp.zeros((NG, CHUNK - TT, NB * GW), bf16)

    def mm(dst, lhs, w_ref, c0=0, r0=0):
        def chunk(n):
            def run():
                a = h_s[slot] if lhs is None else lhs[...]
                dst[r0:r0 + TM, n * GBLK:(n + 1) * GBLK] = jnp.dot(
                    a, w_ref[:, c0 + n * GBLK:c0 + (n + 1) * GBLK],
                    preferred_element_type=f32)
            return run
        return [chunk(n) for n in range(NGB)]

    def proj(dst, c0, r0=0):
        return mm(dst, None, win_ref, c0, r0)

    def interleave(mxu, vpu):
        n, m = len(mxu), len(vpu)
        done = 0
        for k, f in enumerate(mxu):
            upto = (m * (k + 1)) // n
            for g in vpu[done:upto]:
                g()
            done = upto
            f()

    cb = cb_ref[...]
    cw = [cw_ref[k:k + 1, :] for k in range(CONV_W)]

    def p4a(j):
        def run():
            xc = cb
            for k in range(CONV_W):
                xc = xc + cw[k] * xb_s[j * RB + k * NB:(j + 1) * RB + k * NB, :]
            bxc[rows_of(j), :] = xc
            xcb_s[rows_of(j), :] = xc.astype(bf16)
        return run

    def conv_tail():
        tail = xb_s[TM:TM + TAIL, :]
        cn_ref[...] = tail
        xb_s[0:TAIL, :] = tail

    vng = vng_ref[...]

    def p2(j):
        def run():
            vv = bv[rows_of(j), :]
            vn = vv * _rms_scale(vv) * vng
            for g in range(NG):
                slab[g, rows_of(j), :] = vn[:, g * GW:(g + 1) * GW]
        return run

    def relayout(g):
        def run():
            for b in range(NB):
                vprime[g, pl.ds(hrow, TT), b * GW:(b + 1) * GW] = (
                    slab[g, pl.ds(b, TT, stride=NB), :].astype(bf16))
        return run

    def gates(q):
        def run():
            cols = slice(q * GBLK, (q + 1) * GBLK)
            brp[:, cols] = jnp.dot(xcb_s[:, cols], wga_ref[q], preferred_element_type=f32)
            bip[:, cols] = jnp.dot(xcb_s[:, cols], wgx_ref[q], preferred_element_type=f32)
        return run

    t_idx = hrow + lax.broadcasted_iota(jnp.int32, (TT, CHUNK), 0)
    s_idx = lax.broadcasted_iota(jnp.int32, (TT, CHUNK), 1)
    causal = s_idx <= t_idx

    def spatial(g):
        def run():
            wt = jnp.where(causal, ws_ref[g, pl.ds(hrow, TT), :], jnp.zeros((), bf16))
            sp = jnp.dot(wt, vprime[g], preferred_element_type=f32)
            bias = bs_ref[g, pl.ds(hrow, TT), :]
            for b in range(NB):
                slab[g, pl.ds(b, TT, stride=NB), :] = sp[:, b * GW:(b + 1) * GW] + bias
        return run

    def p3(j):
        def run():
            for g in range(NG):
                cols = slice(g * GW, (g + 1) * GW)
                ya_s[rows_of(j), cols] = (
                    bu[rows_of(j), cols] * slab[g, rows_of(j), :]
                    * _silu(bga[rows_of(j), cols])).astype(bf16)
        return run

    ba = brg_ref[0:1, :]
    bx = brg_ref[1:2, :]
    cneg = _log2_decay(lam_ref[...])
    state = [hst[...]]

    def p4b(j):
        def run():
            xc = bxc[rows_of(j), :]
            r = _sigmoid(brp[rows_of(j), :] + ba)
            ig = _sigmoid(bip[rows_of(j), :] + bx)
            a = jnp.exp2(r * cneg)
            xs = _sqrt_unit_interval(1.0 - a * a) * (ig * xc)
            h = state[0]
            ys = []
            for t in range(RB // NB):
                h = a[t * NB:(t + 1) * NB] * h + xs[t * NB:(t + 1) * NB]
                ys.append(h)
            state[0] = h
            yr = jnp.concatenate(ys, axis=0)
            yb_s[rows_of(j), :] = (yr * bgb[rows_of(j), :]).astype(bf16)
        return run

    def gb_act(q):
        def run():
            cols = slice(q * GBLK, (q + 1) * GBLK)
            bgb[:, cols] = _silu(bgb[:, cols])
        return run

    def scan_done():
        hst[...] = state[0]
        hl_ref[...] = state[0]

    def p5z(j):
        def run():
            bv[rows_of(j), :] = _sigmoid(bv[rows_of(j), :])
            bpa[rows_of(j), :] = _sigmoid(bpa[rows_of(j), :])
        return run

    def p5a(j):
        def run():
            bu[rows_of(j), :] = bv[rows_of(j), :] * bu[rows_of(j), :]
        return run

    def p5b(q):
        def run():
            cols = slice(q * GBLK, (q + 1) * GBLK)
            for j in range(NRB):
                xcb_s[rows_of(j), cols] = (
                    bu[rows_of(j), cols] + bpa[rows_of(j), cols] * brp[rows_of(j), cols]
                ).astype(bf16)
        return run

    fg = fg_ref[...]

    def p6(j):
        def run():
            o = bgb[rows_of(j), :].reshape(8, NB, D)
            y = x_ref[j * 8:(j + 1) * 8] + gate[None] * o
            if final:
                y = y * _rms_scale(y) * fg
            out_ref[j * 8:(j + 1) * 8] = y
        return run

    def units(f):
        return [f(j) for j in range(NRB)]

    interleave(proj(xb_s, C_XB, TAIL), [])
    interleave(proj(bgb, C_GB), units(p4a) + [conv_tail])
    interleave(proj(bu, C_U), [gb_act(q) for q in range(NGB)])
    interleave([gates(q) for q in range(NGB)], [])
    interleave(proj(bv, C_V), units(p4b)[:NRB // 2])
    interleave(proj(bga, C_GA), units(p4b)[NRB // 2:] + [scan_done])
    interleave(proj(bpa, C_ZB), units(p2) + [relayout(g) for g in range(NG)])
    interleave([spatial(g) for g in range(NG)], [])
    interleave(proj(bv, C_ZA), units(p3))
    interleave(mm(bu, ya_s, wpa_ref), units(p5z) + prenorm(xn_ref, 1 - slot))
    interleave(mm(brp, yb_s, wpb_ref), units(p5a))
    interleave([p5b(q) for q in range(NGB)], [])
    interleave(mm(bgb, xcb_s, wout_ref), [])
    for f in units(p6):
        f()


def _layer_const_spec(shape, l):
    nd = len(shape)
    return pl.BlockSpec((None,) + tuple(shape[1:]), lambda i: (l,) + (0,) * (nd - 1),
                        pipeline_mode=pl.Buffered(1))


def _prompt_layer(l, x_tb, mod, ng, win, vng, ws, bs, cw, cb, wga, wgx, brg, lam,
                  wpa, wpb, wout, fg):
    final = l == DEPTH - 1
    stacked = (ng, win, vng, ws, bs, cw, cb, wga, wgx, brg, lam, wpa, wpb, wout)
    mod_spec = pl.BlockSpec((None, NB, 3 * D), lambda i: (l, NS // NB, 0),
                            pipeline_mode=pl.Buffered(1))
    fg_spec = pl.BlockSpec((1, D), lambda i: (0, 0), pipeline_mode=pl.Buffered(1))
    return pl.pallas_call(
        functools.partial(_prompt_layer_kernel, final),
        grid=(NT,),
        in_specs=[pl.BlockSpec((TT, NB, D), lambda i: (i, 0, 0)),
                  pl.BlockSpec((TT, NB, D), lambda i: (jnp.minimum(i + 1, NT - 1), 0, 0))]
        + [mod_spec] + [_layer_const_spec(a.shape, l) for a in stacked] + [fg_spec],
        out_specs=[
            pl.BlockSpec((TT, NB, D), lambda i: (i, 0, 0)),
            pl.BlockSpec((NB, D), lambda i: (0, 0)),
            pl.BlockSpec((TAIL, D), lambda i: (0, 0)),
        ],
        out_shape=[
            jax.ShapeDtypeStruct((SEQ, NB, D), f32),
            jax.ShapeDtypeStruct((NB, D), f32),
            jax.ShapeDtypeStruct((TAIL, D), f32),
        ],
        scratch_shapes=[
            pltpu.VMEM((2, TM, D), bf16),
            pltpu.VMEM((TM, D), f32),
            pltpu.VMEM((TM, D), f32),
            pltpu.VMEM((TM, D), f32),
            pltpu.VMEM((TM, D), f32),
            pltpu.VMEM((TM, D), f32),
            pltpu.VMEM((TM, D), f32),
            pltpu.VMEM((TM, D), f32),
            pltpu.VMEM((TM, D), f32),
            pltpu.VMEM((NG, TM, GW), f32),
            pltpu.VMEM((NG, CHUNK, NB * GW), bf16),
            pltpu.VMEM((TM + TAIL, D), f32),
            pltpu.VMEM((NB, D), f32),
            pltpu.VMEM((TM, D), bf16),
            pltpu.VMEM((TM, D), bf16),
            pltpu.VMEM((TM, D), bf16),
        ],
        compiler_params=pltpu.CompilerParams(
            dimension_semantics=("arbitrary",), vmem_limit_bytes=VMEM_LIMIT),
        name="prompt_layer_final" if final else "prompt_layer",
    )(x_tb, x_tb, mod, *stacked, fg)


def _sample_kernel(x_ref, mod_ref, h0_ref, cbuf_ref, ng_ref, win_ref, vng_ref, ws0_ref,
                   bs0_ref, cw_ref, cb_ref, wga_ref, wgx_ref, brg_ref, lam_ref,
                   wpa_ref, wpb_ref, wout_ref, fg_ref,
                   y_ref, hs_ref, cs_ref, vs_ref, xs_s):
    l = pl.program_id(0)

    @pl.when(l == 0)
    def _():
        xs_s[...] = x_ref[...]

    x = xs_s[...]
    shift = mod_ref[0, :, 0:D]
    scale = mod_ref[0, :, D:2 * D]
    gate = mod_ref[0, :, 2 * D:3 * D]
    h = ((x * _rms_scale(x) * ng_ref[0]) * (1.0 + scale) + shift).astype(bf16)

    def proj(c0):
        return jnp.dot(h, win_ref[0, :, c0:c0 + D], preferred_element_type=f32)

    v = proj(C_V)
    v = v * _rms_scale(v) * vng_ref[0]
    vs_ref[0] = v
    s = ws0_ref[0] * v + bs0_ref[0]
    ya = (proj(C_U) * s * _silu(proj(C_GA))).astype(bf16)

    xb = proj(C_XB)
    xc = (cb_ref[0] + cw_ref[0, 0:1, :] * cbuf_ref[0, 0] + cw_ref[0, 1:2, :] * cbuf_ref[0, 1]
          + cw_ref[0, 2:3, :] * cbuf_ref[0, 2] + cw_ref[0, 3:4, :] * xb)
    cs_ref[0, 0] = cbuf_ref[0, 1]
    cs_ref[0, 1] = cbuf_ref[0, 2]
    cs_ref[0, 2] = xb
    xcb = xc.astype(bf16)
    rp = jnp.concatenate(
        [jnp.dot(xcb[:, q * GBLK:(q + 1) * GBLK], wga_ref[0, q], preferred_element_type=f32)
         for q in range(NGB)], axis=1)
    ip = jnp.concatenate(
        [jnp.dot(xcb[:, q * GBLK:(q + 1) * GBLK], wgx_ref[0, q], preferred_element_type=f32)
         for q in range(NGB)], axis=1)
    r = _sigmoid(rp + brg_ref[0, 0:1, :])
    ig = _sigmoid(ip + brg_ref[0, 1:2, :])
    a = jnp.exp2(r * _log2_decay(lam_ref[0]))
    hn = a * h0_ref[0] + _sqrt_unit_interval(1.0 - a * a) * (ig * xc)
    hs_ref[0] = hn
    yb = (hn * _silu(proj(C_GB))).astype(bf16)

    pa = jnp.dot(ya, wpa_ref[0], preferred_element_type=f32)
    pb = jnp.dot(yb, wpb_ref[0], preferred_element_type=f32)
    mg = (_sigmoid(proj(C_ZA)) * pa + _sigmoid(proj(C_ZB)) * pb).astype(bf16)
    xn = x + gate * jnp.dot(mg, wout_ref[0], preferred_element_type=f32)
    xs_s[...] = xn

    @pl.when(l == DEPTH - 1)
    def _():
        y_ref[...] = xn * _rms_scale(xn) * fg_ref[...]


def _layer_spec(shape):
    nd = len(shape)
    return pl.BlockSpec((1,) + tuple(shape[1:]), lambda l: (l,) + (0,) * (nd - 1))


def _sample_group(x_s, mod_s, h0, cbuf, ng, win, vng, ws0, bs0, cw, cb, wga, wgx, brg, lam,
                  wpa, wpb, wout, fg):
    per_layer = (mod_s, h0, cbuf, ng, win, vng, ws0, bs0, cw, cb, wga, wgx, brg, lam,
                 wpa, wpb, wout)
    return pl.pallas_call(
        _sample_kernel,
        grid=(DEPTH,),
        in_specs=[pl.BlockSpec((NS, D), lambda l: (0, 0)),
                  pl.BlockSpec((1, NS, 3 * D), lambda l: (l, 0, 0))]
        + [_layer_spec(a.shape) for a in per_layer[1:]]
        + [pl.BlockSpec((1, D), lambda l: (0, 0))],
        out_specs=[
            pl.BlockSpec((NS, D), lambda l: (0, 0)),
            pl.BlockSpec((1, NS, D), lambda l: (l, 0, 0)),
            pl.BlockSpec((1, CONV_W - 1, NS, D), lambda l: (l, 0, 0, 0)),
            pl.BlockSpec((1, NS, D), lambda l: (l, 0, 0)),
        ],
        out_shape=[
            jax.ShapeDtypeStruct((NS, D), f32),
            jax.ShapeDtypeStruct((DEPTH, NS, D), f32),
            jax.ShapeDtypeStruct((DEPTH, CONV_W - 1, NS, D), f32),
            jax.ShapeDtypeStruct((DEPTH, NS, D), f32),
        ],
        scratch_shapes=[pltpu.VMEM((NS, D), f32)],
        compiler_params=pltpu.CompilerParams(
            dimension_semantics=("arbitrary",), vmem_limit_bytes=VMEM_LIMIT),
        name="sample_layers",
    )(x_s, *per_layer, fg)


def _block_diag_gate(w):
    per = GBLK // BW
    w = w.reshape(DEPTH, NGB, per, BW, BW)
    eye = jnp.eye(per, dtype=w.dtype)
    out = jnp.einsum('lqhij,hk->lqhikj', w, eye)
    return out.reshape(DEPTH, NGB, GBLK, GBLK)


def kernel(x_prompt, x_sample, c_prompt, c_sample, state_rglru_h, state_conv, w_ada, b_ada,
           norm_g, w_in, v_norm_g, w_s, b_s, conv_w, conv_b, w_rg_a, b_rg_a, w_rg_x,
           b_rg_x, lam, w_pa, w_pb, w_out, final_g):
    c_all = jnp.concatenate([c_sample, c_prompt, jnp.zeros((8, D), f32)], axis=0)
    mod = _modulation(c_all, w_ada, b_ada)

    win = w_in.astype(bf16)
    wpa = w_pa.astype(bf16)
    wpb = w_pb.astype(bf16)
    wout = w_out.astype(bf16)
    wga = _block_diag_gate(w_rg_a).astype(bf16)
    wgx = _block_diag_gate(w_rg_x).astype(bf16)
    ws = w_s.astype(bf16)
    bs = jnp.broadcast_to(b_s[..., None], (DEPTH, NG, CHUNK, GW))
    ng = norm_g.reshape(DEPTH, 1, D)
    vng = v_norm_g.reshape(DEPTH, 1, D)
    cb = conv_b.reshape(DEPTH, 1, D)
    brg = jnp.stack([b_rg_a, b_rg_x], axis=1)
    lam3 = lam.reshape(DEPTH, 1, D)
    fg = final_g.reshape(1, D)

    x_tb = x_prompt.transpose(1, 0, 2)
    hp, cp = [], []
    for l in range(DEPTH):
        x_tb, h_l, c_l = _prompt_layer(
            l, x_tb, mod, ng, win, vng, ws, bs, conv_w, cb, wga, wgx, brg, lam3,
            wpa, wpb, wout, fg)
        hp.append(h_l)
        cp.append(c_l.reshape(CONV_W - 1, NB, D).transpose(1, 0, 2))
    y_prompt = x_tb.transpose(1, 0, 2)
    h_prompt = jnp.stack(hp)
    conv_prompt = jnp.stack(cp)

    ws0 = jnp.repeat(w_s[:, :, 0, 0], GW, axis=-1).reshape(DEPTH, 1, D)
    bs0 = jnp.repeat(b_s[:, :, 0], GW, axis=-1).reshape(DEPTH, 1, D)
    cbuf = state_conv.transpose(0, 2, 1, 3)
    y_s, h_sample, cs, vs = _sample_group(
        x_sample.reshape(NS, D), mod, state_rglru_h, cbuf, ng, win, vng, ws0, bs0,
        conv_w, cb, wga, wgx, brg, lam3, wpa, wpb, wout, fg)
    y_sample = y_s.reshape(NS, 1, D)
    conv_sample = cs.transpose(0, 2, 1, 3)
    chunk_v_sample = vs.reshape(DEPTH, NS, 1, D)
    return (y_prompt, y_sample, h_prompt, conv_prompt, h_sample, conv_sample,
            chunk_v_sample)
```

```python
import functools

import jax
import jax.numpy as jnp
from jax import lax
from jax.experimental import pallas as pl
from jax.experimental.pallas import tpu as pltpu

D = 1024
NB = 8
SEQ = 2048
DEPTH = 4
NS = 128
CHUNK = 128
GW = 128
NG = D // GW
HB = 16
BW = D // HB
CONV_W = 4
C_RG = 8.0
EPS = 1e-6
D_IN = 7 * D
C_U, C_V, C_GA, C_XB, C_GB, C_ZA, C_ZB = (i * D for i in range(7))

TT = 32
TM = TT * NB
NT = SEQ // TT
RB = 64
NRB = TM // RB
GBLK = 256
NGB = D // GBLK
TAIL = (CONV_W - 1) * NB

VMEM_LIMIT = 62 * 1024 * 1024

f32 = jnp.float32
bf16 = jnp.bfloat16


def _sigmoid(x):
    return 0.5 * jnp.tanh(0.5 * x) + 0.5


def _silu(x):
    return x * _sigmoid(x)


def _sqrt_unit_interval(x):
    return jnp.where(x == 0.0, 0.0, x * lax.rsqrt(x))


LOG2E = 1.4426950408889634


def _log2_decay(lam):
    y = -lam
    sp = jnp.maximum(y, 0.0) + jnp.log1p(jnp.exp(-jnp.abs(y)))
    return (-C_RG * LOG2E) * sp


def _rms_scale(x):
    return lax.rsqrt(jnp.mean(x * x, axis=-1, keepdims=True) + EPS)


def _mod_kernel(c_ref, w_ref, b_ref, o_ref):
    a = _silu(c_ref[...]).astype(bf16)
    w = w_ref[0].astype(bf16)
    o_ref[0] = jnp.dot(a, w, preferred_element_type=f32) + b_ref[0]


def _modulation(c_all, w_ada, b_ada):
    m = c_all.shape[0]
    return pl.pallas_call(
        _mod_kernel,
        grid=(DEPTH, 3),
        in_specs=[
            pl.BlockSpec((m, D), lambda l, j: (0, 0)),
            pl.BlockSpec((1, D, D), lambda l, j: (l, 0, j)),
            pl.BlockSpec((1, 1, D), lambda l, j: (l, 0, j)),
        ],
        out_specs=pl.BlockSpec((1, m, D), lambda l, j: (l, 0, j)),
        out_shape=jax.ShapeDtypeStruct((DEPTH, m, 3 * D), f32),
        compiler_params=pltpu.CompilerParams(
            dimension_semantics=("arbitrary", "arbitrary")),
        name="adaln_mod",
    )(c_all, w_ada, b_ada.reshape(DEPTH, 1, 3 * D))


def _prompt_layer_kernel(final, first, *refs):
    n_x = 2 * NB if first else 2
    x_refs, refs = refs[:n_x], refs[n_x:]
    (mod_ref, ng_ref, win_ref, vng_ref, ws_ref, bs_ref, cw_ref, cb_ref, wga_ref, wgx_ref,
     brg_ref, lam_ref, wpa_ref, wpb_ref, wout_ref, fg_ref,
     out_ref, hl_ref, cn_ref,
     h_s, bv, bu, bga, bgb, bxc, brp, bip, bpa, slab, vprime, xb_s,
     hst, ya_s, yb_s, xcb_s) = refs[:35]
    stag = refs[35] if first else None
    i = pl.program_id(0)
    part = i % (CHUNK // TT)
    hrow = pl.multiple_of(part * TT, TT)
    slot = i % 2

    shift = mod_ref[:, 0:D]
    scale1 = 1.0 + mod_ref[:, D:2 * D]
    gate = mod_ref[:, 2 * D:3 * D]
    ng = ng_ref[...]

    def rows_of(j):
        return slice(j * RB, (j + 1) * RB)

    def stage(batch_refs, dst_slot):
        def unit(b):
            def run():
                for c in range(D // GW):
                    stag[dst_slot, c, pl.ds(b, TT, stride=NB), :] = (
                        batch_refs[b][:, c * GW:(c + 1) * GW])
            return run
        return [unit(b) for b in range(NB)]

    def load_x(which, j):
        if not first:
            return x_refs[which][j * 8:(j + 1) * 8]
        s = slot if which == 0 else 1 - slot
        xv = jnp.concatenate([stag[s, c, rows_of(j), :] for c in range(D // GW)], axis=1)
        return xv.reshape(8, NB, D)

    def prenorm(which, dst_slot):
        def unit(j):
            def run():
                xv = load_x(which, j)
                hv = (xv * _rms_scale(xv) * ng) * scale1[None] + shift[None]
                h_s[dst_slot, rows_of(j), :] = hv.reshape(RB, D).astype(bf16)
            return run
        return [unit(j) for j in range(NRB)]

    @pl.when(i == 0)
    def _():
        hst[...] = jnp.zeros_like(hst)
        xb_s[0:TAIL, :] = jnp.zeros((TAIL, D), f32)
        if first:
            for f in stage(x_refs[:NB], 0):
                f()
        for f in prenorm(0, 0):
            f()

    @pl.when(part == 0)
    def _():
        vprime[:, TT:CHUNK, :] = jnp.zeros((NG, CHUNK - TT, NB * GW), bf16)

    def mm(dst, lhs, w_ref, c0=0, r0=0):
        def chunk(n):
            def run():
                a = h_s[slot] if lhs is None else lhs[...]
                dst[r0:r0 + TM, n * GBLK:(n + 1) * GBLK] = jnp.dot(
                    a, w_ref[:, c0 + n * GBLK:c0 + (n + 1) * GBLK],
                    preferred_element_type=f32)
            return run
        return [chunk(n) for n in range(NGB)]

    def proj(dst, c0, r0=0):
        return mm(dst, None, win_ref, c0, r0)

    def interleave(mxu, vpu):
        n, m = len(mxu), len(vpu)
        done = 0
        for k, f in enumerate(mxu):
            upto = (m * (k + 1)) // n
            for g in vpu[done:upto]:
                g()
            done = upto
            f()

    cb = cb_ref[...]
    cw = [cw_ref[k:k + 1, :] for k in range(CONV_W)]

    def p4a(j):
        def run():
            xc = cb
            for k in range(CONV_W):
                xc = xc + cw[k] * xb_s[j * RB + k * NB:(j + 1) * RB + k * NB, :]
            bxc[rows_of(j), :] = xc
            xcb_s[rows_of(j), :] = xc.astype(bf16)
        return run

    def conv_tail():
        tail = xb_s[TM:TM + TAIL, :]
        cn_ref[...] = tail
        xb_s[0:TAIL, :] = tail

    vng = vng_ref[...]

    def p2(j):
        def run():
            vv = bv[rows_of(j), :]
            vn = vv * _rms_scale(vv) * vng
            for g in range(NG):
                slab[g, rows_of(j), :] = vn[:, g * GW:(g + 1) * GW]
        return run

    def relayout(g):
        def run():
            for b in range(NB):
                vprime[g, pl.ds(hrow, TT), b * GW:(b + 1) * GW] = (
                    slab[g, pl.ds(b, TT, stride=NB), :].astype(bf16))
        return run

    def gates(q):
        def run():
            cols = slice(q * GBLK, (q + 1) * GBLK)
            brp[:, cols] = jnp.dot(xcb_s[:, cols], wga_ref[q], preferred_element_type=f32)
            bip[:, cols] = jnp.dot(xcb_s[:, cols], wgx_ref[q], preferred_element_type=f32)
        return run

    t_idx = hrow + lax.broadcasted_iota(jnp.int32, (TT, CHUNK), 0)
    s_idx = lax.broadcasted_iota(jnp.int32, (TT, CHUNK), 1)
    causal = s_idx <= t_idx

    def spatial(g):
        def run():
            wt = jnp.where(causal, ws_ref[g, pl.ds(hrow, TT), :], jnp.zeros((), bf16))
            sp = jnp.dot(wt, vprime[g], preferred_element_type=f32)
            bias = bs_ref[g, pl.ds(hrow, TT), :]
            for b in range(NB):
                slab[g, pl.ds(b, TT, stride=NB), :] = sp[:, b * GW:(b + 1) * GW] + bias
        return run

    def p3(j):
        def run():
            for g in range(NG):
                cols = slice(g * GW, (g + 1) * GW)
                ya_s[rows_of(j), cols] = (
                    bu[rows_of(j), cols] * slab[g, rows_of(j), :]
                    * _silu(bga[rows_of(j), cols])).astype(bf16)
        return run

    ba = brg_ref[0:1, :]
    bx = brg_ref[1:2, :]
    cneg = _log2_decay(lam_ref[...])
    state = [hst[...]]

    def p4b(j):
        def run():
            xc = bxc[rows_of(j), :]
            r = _sigmoid(brp[rows_of(j), :] + ba)
            ig = _sigmoid(bip[rows_of(j), :] + bx)
            a = jnp.exp2(r * cneg)
            xs = _sqrt_unit_interval(1.0 - a * a) * (ig * xc)
            h = state[0]
            ys = []
            for t in range(RB // NB):
                h = a[t * NB:(t + 1) * NB] * h + xs[t * NB:(t + 1) * NB]
                ys.append(h)
            state[0] = h
            yr = jnp.concatenate(ys, axis=0)
            yb_s[rows_of(j), :] = (yr * bgb[rows_of(j), :]).astype(bf16)
        return run

    def gb_act(q):
        def run():
            cols = slice(q * GBLK, (q + 1) * GBLK)
            bgb[:, cols] = _silu(bgb[:, cols])
        return run

    def scan_done():
        hst[...] = state[0]
        hl_ref[...] = state[0]

    def p5z(j):
        def run():
            bv[rows_of(j), :] = _sigmoid(bv[rows_of(j), :])
            bpa[rows_of(j), :] = _sigmoid(bpa[rows_of(j), :])
        return run

    def p5a(j):
        def run():
            bu[rows_of(j), :] = bv[rows_of(j), :] * bu[rows_of(j), :]
        return run

    def p5b(q):
        def run():
            cols = slice(q * GBLK, (q + 1) * GBLK)
            for j in range(NRB):
                xcb_s[rows_of(j), cols] = (
                    bu[rows_of(j), cols] + bpa[rows_of(j), cols] * brp[rows_of(j), cols]
                ).astype(bf16)
        return run

    fg = fg_ref[...]

    def p6(j):
        def run():
            o = bgb[rows_of(j), :].reshape(8, NB, D)
            y = load_x(0, j) + gate[None] * o
            if final:
                y = y * _rms_scale(y) * fg
            out_ref[j * 8:(j + 1) * 8] = y
        return run

    def units(f):
        return [f(j) for j in range(NRB)]

    interleave(proj(xb_s, C_XB, TAIL), stage(x_refs[NB:], 1 - slot) if first else [])
    interleave(proj(bgb, C_GB), units(p4a) + [conv_tail])
    interleave(proj(bu, C_U), [gb_act(q) for q in range(NGB)])
    interleave([gates(q) for q in range(NGB)], [])
    interleave(proj(bv, C_V), units(p4b)[:NRB // 2])
    interleave(proj(bga, C_GA), units(p4b)[NRB // 2:] + [scan_done])
    interleave(proj(bpa, C_ZB), units(p2) + [relayout(g) for g in range(NG)])
    interleave([spatial(g) for g in range(NG)], [])
    interleave(proj(bv, C_ZA), units(p3))
    interleave(mm(bu, ya_s, wpa_ref), units(p5z) + prenorm(1, 1 - slot))
    interleave(mm(brp, yb_s, wpb_ref), units(p5a))
    interleave([p5b(q) for q in range(NGB)], [])
    interleave(mm(bgb, xcb_s, wout_ref), [])
    for f in units(p6):
        f()


def _layer_const_spec(shape, l):
    nd = len(shape)
    return pl.BlockSpec((None,) + tuple(shape[1:]), lambda i: (l,) + (0,) * (nd - 1),
                        pipeline_mode=pl.Buffered(1))


def _prompt_layer(l, x_in, mod, ng, win, vng, ws, bs, cw, cb, wga, wgx, brg, lam,
                  wpa, wpb, wout, fg):
    final = l == DEPTH - 1
    stacked = (ng, win, vng, ws, bs, cw, cb, wga, wgx, brg, lam, wpa, wpb, wout)
    mod_spec = pl.BlockSpec((None, NB, 3 * D), lambda i: (l, NS // NB, 0),
                            pipeline_mode=pl.Buffered(1))
    fg_spec = pl.BlockSpec((1, D), lambda i: (0, 0), pipeline_mode=pl.Buffered(1))
    first = l == 0
    if first:
        x_specs = ([pl.BlockSpec((None, TT, D), lambda i, b=b: (b, 0, 0),
                                 pipeline_mode=pl.Buffered(1)) for b in range(NB)]
                   + [pl.BlockSpec((None, TT, D),
                                   lambda i, b=b: (b, jnp.minimum(i + 1, NT - 1), 0))
                      for b in range(NB)])
        extra_scratch = [pltpu.VMEM((2, D // GW, TM, GW), f32)]
    else:
        x_specs = [pl.BlockSpec((TT, NB, D), lambda i: (i, 0, 0)),
                   pl.BlockSpec((TT, NB, D), lambda i: (jnp.minimum(i + 1, NT - 1), 0, 0))]
        extra_scratch = []
    return pl.pallas_call(
        functools.partial(_prompt_layer_kernel, final, first),
        grid=(NT,),
        in_specs=x_specs
        + [mod_spec] + [_layer_const_spec(a.shape, l) for a in stacked] + [fg_spec],
        out_specs=[
            pl.BlockSpec((TT, NB, D), lambda i: (i, 0, 0)),
            pl.BlockSpec((NB, D), lambda i: (0, 0)),
            pl.BlockSpec((TAIL, D), lambda i: (0, 0)),
        ],
        out_shape=[
            jax.ShapeDtypeStruct((SEQ, NB, D), f32),
            jax.ShapeDtypeStruct((NB, D), f32),
            jax.ShapeDtypeStruct((TAIL, D), f32),
        ],
        scratch_shapes=[
            pltpu.VMEM((2, TM, D), bf16),
            pltpu.VMEM((TM, D), f32),
            pltpu.VMEM((TM, D), f32),
            pltpu.VMEM((TM, D), f32),
            pltpu.VMEM((TM, D), f32),
            pltpu.VMEM((TM, D), f32),
            pltpu.VMEM((TM, D), f32),
            pltpu.VMEM((TM, D), f32),
            pltpu.VMEM((TM, D), f32),
            pltpu.VMEM((NG, TM, GW), f32),
            pltpu.VMEM((NG, CHUNK, NB * GW), bf16),
            pltpu.VMEM((TM + TAIL, D), f32),
            pltpu.VMEM((NB, D), f32),
            pltpu.VMEM((TM, D), bf16),
            pltpu.VMEM((TM, D), bf16),
            pltpu.VMEM((TM, D), bf16),
        ] + extra_scratch,
        compiler_params=pltpu.CompilerParams(
            dimension_semantics=("arbitrary",), vmem_limit_bytes=VMEM_LIMIT),
        name="prompt_layer_final" if final else ("prompt_layer_first" if first
                                                 else "prompt_layer"),
    )(*([x_in] * len(x_specs)), mod, *stacked, fg)


def _sample_kernel(x_ref, mod_ref, h0_ref, cbuf_ref, ng_ref, win_ref, vng_ref, ws0_ref,
                   bs0_ref, cw_ref, cb_ref, wga_ref, wgx_ref, brg_ref, lam_ref,
                   wpa_ref, wpb_ref, wout_ref, fg_ref,
                   y_ref, hs_ref, cs_ref, vs_ref, xs_s):
    l = pl.program_id(0)

    @pl.when(l == 0)
    def _():
        xs_s[...] = x_ref[...]

    x = xs_s[...]
    shift = mod_ref[0, :, 0:D]
    scale = mod_ref[0, :, D:2 * D]
    gate = mod_ref[0, :, 2 * D:3 * D]
    h = ((x * _rms_scale(x) * ng_ref[0]) * (1.0 + scale) + shift).astype(bf16)

    def proj(c0):
        return jnp.dot(h, win_ref[0, :, c0:c0 + D], preferred_element_type=f32)

    v = proj(C_V)
    v = v * _rms_scale(v) * vng_ref[0]
    vs_ref[0] = v
    s = ws0_ref[0] * v + bs0_ref[0]
    ya = (proj(C_U) * s * _silu(proj(C_GA))).astype(bf16)

    xb = proj(C_XB)
    xc = (cb_ref[0] + cw_ref[0, 0:1, :] * cbuf_ref[0, 0] + cw_ref[0, 1:2, :] * cbuf_ref[0, 1]
          + cw_ref[0, 2:3, :] * cbuf_ref[0, 2] + cw_ref[0, 3:4, :] * xb)
    cs_ref[0, 0] = cbuf_ref[0, 1]
    cs_ref[0, 1] = cbuf_ref[0, 2]
    cs_ref[0, 2] = xb
    xcb = xc.astype(bf16)
    rp = jnp.concatenate(
        [jnp.dot(xcb[:, q * GBLK:(q + 1) * GBLK], wga_ref[0, q], preferred_element_type=f32)
         for q in range(NGB)], axis=1)
    ip = jnp.concatenate(
        [jnp.dot(xcb[:, q * GBLK:(q + 1) * GBLK], wgx_ref[0, q], preferred_element_type=f32)
         for q in range(NGB)], axis=1)
    r = _sigmoid(rp + brg_ref[0, 0:1, :])
    ig = _sigmoid(ip + brg_ref[0, 1:2, :])
    a = jnp.exp2(r * _log2_decay(lam_ref[0]))
    hn = a * h0_ref[0] + _sqrt_unit_interval(1.0 - a * a) * (ig * xc)
    hs_ref[0] = hn
    yb = (hn * _silu(proj(C_GB))).astype(bf16)

    pa = jnp.dot(ya, wpa_ref[0], preferred_element_type=f32)
    pb = jnp.dot(yb, wpb_ref[0], preferred_element_type=f32)
    mg = (_sigmoid(proj(C_ZA)) * pa + _sigmoid(proj(C_ZB)) * pb).astype(bf16)
    xn = x + gate * jnp.dot(mg, wout_ref[0], preferred_element_type=f32)
    xs_s[...] = xn

    @pl.when(l == DEPTH - 1)
    def _():
        y_ref[...] = xn * _rms_scale(xn) * fg_ref[...]


def _layer_spec(shape):
    nd = len(shape)
    return pl.BlockSpec((1,) + tuple(shape[1:]), lambda l: (l,) + (0,) * (nd - 1))


def _sample_group(x_s, mod_s, h0, cbuf, ng, win, vng, ws0, bs0, cw, cb, wga, wgx, brg, lam,
                  wpa, wpb, wout, fg):
    per_layer = (mod_s, h0, cbuf, ng, win, vng, ws0, bs0, cw, cb, wga, wgx, brg, lam,
                 wpa, wpb, wout)
    return pl.pallas_call(
        _sample_kernel,
        grid=(DEPTH,),
        in_specs=[pl.BlockSpec((NS, D), lambda l: (0, 0)),
                  pl.BlockSpec((1, NS, 3 * D), lambda l: (l, 0, 0))]
        + [_layer_spec(a.shape) for a in per_layer[1:]]
        + [pl.BlockSpec((1, D), lambda l: (0, 0))],
        out_specs=[
            pl.BlockSpec((NS, D), lambda l: (0, 0)),
            pl.BlockSpec((1, NS, D), lambda l: (l, 0, 0)),
            pl.BlockSpec((1, CONV_W - 1, NS, D), lambda l: (l, 0, 0, 0)),
            pl.BlockSpec((1, NS, D), lambda l: (l, 0, 0)),
        ],
        out_shape=[
            jax.ShapeDtypeStruct((NS, D), f32),
            jax.ShapeDtypeStruct((DEPTH, NS, D), f32),
            jax.ShapeDtypeStruct((DEPTH, CONV_W - 1, NS, D), f32),
            jax.ShapeDtypeStruct((DEPTH, NS, D), f32),
        ],
        scratch_shapes=[pltpu.VMEM((NS, D), f32)],
        compiler_params=pltpu.CompilerParams(
            dimension_semantics=("arbitrary",), vmem_limit_bytes=VMEM_LIMIT),
        name="sample_layers",
    )(x_s, *per_layer, fg)


def _block_diag_gate(w):
    per = GBLK // BW
    w = w.reshape(DEPTH, NGB, per, BW, BW)
    eye = jnp.eye(per, dtype=w.dtype)
    out = jnp.einsum('lqhij,hk->lqhikj', w, eye)
    return out.reshape(DEPTH, NGB, GBLK, GBLK)


def kernel(x_prompt, x_sample, c_prompt, c_sample, state_rglru_h, state_conv, w_ada, b_ada,
           norm_g, w_in, v_norm_g, w_s, b_s, conv_w, conv_b, w_rg_a, b_rg_a, w_rg_x,
           b_rg_x, lam, w_pa, w_pb, w_out, final_g):
    c_all = jnp.concatenate([c_sample, c_prompt, jnp.zeros((8, D), f32)], axis=0)
    mod = _modulation(c_all, w_ada, b_ada)

    win = w_in.astype(bf16)
    wpa = w_pa.astype(bf16)
    wpb = w_pb.astype(bf16)
    wout = w_out.astype(bf16)
    wga = _block_diag_gate(w_rg_a).astype(bf16)
    wgx = _block_diag_gate(w_rg_x).astype(bf16)
    ws = w_s.astype(bf16)
    bs = jnp.broadcast_to(b_s[..., None], (DEPTH, NG, CHUNK, GW))
    ng = norm_g.reshape(DEPTH, 1, D)
    vng = v_norm_g.reshape(DEPTH, 1, D)
    cb = conv_b.reshape(DEPTH, 1, D)
    brg = jnp.stack([b_rg_a, b_rg_x], axis=1)
    lam3 = lam.reshape(DEPTH, 1, D)
    fg = final_g.reshape(1, D)

    x_tb = x_prompt
    hp, cp = [], []
    for l in range(DEPTH):
        x_tb, h_l, c_l = _prompt_layer(
            l, x_tb, mod, ng, win, vng, ws, bs, conv_w, cb, wga, wgx, brg, lam3,
            wpa, wpb, wout, fg)
        hp.append(h_l)
        cp.append(c_l.reshape(CONV_W - 1, NB, D).transpose(1, 0, 2))
    y_prompt = x_tb.transpose(1, 0, 2)
    h_prompt = jnp.stack(hp)
    conv_prompt = jnp.stack(cp)

    ws0 = jnp.repeat(w_s[:, :, 0, 0], GW, axis=-1).reshape(DEPTH, 1, D)
    bs0 = jnp.repeat(b_s[:, :, 0], GW, axis=-1).reshape(DEPTH, 1, D)
    cbuf = state_conv.transpose(0, 2, 1, 3)
    y_s, h_sample, cs, vs = _sample_group(
        x_sample.reshape(NS, D), mod, state_rglru_h, cbuf, ng, win, vng, ws0, bs0,
        conv_w, cb, wga, wgx, brg, lam3, wpa, wpb, wout, fg)
    y_sample = y_s.reshape(NS, 1, D)
    conv_sample = cs.transpose(0, 2, 1, 3)
    chunk_v_sample = vs.reshape(DEPTH, NS, 1, D)
    return (y_prompt, y_sample, h_prompt, conv_prompt, h_sample, conv_sample,
            chunk_v_sample)
```

```python
import functools

import jax
import jax.numpy as jnp
from jax import lax
from jax.experimental import pallas as pl
from jax.experimental.pallas import tpu as pltpu

D = 1024
NB = 8
SEQ = 2048
DEPTH = 4
NS = 128
CHUNK = 128
GW = 128
NG = D // GW
HB = 16
BW = D // HB
CONV_W = 4
C_RG = 8.0
EPS = 1e-6
D_IN = 7 * D
C_U, C_V, C_GA, C_XB, C_GB, C_ZA, C_ZB = (i * D for i in range(7))

TT = 32
TM = TT * NB
NT = SEQ // TT
RB = 64
NRB = TM // RB
GBLK = 256
NGB = D // GBLK
TAIL = (CONV_W - 1) * NB

VMEM_LIMIT = 62 * 1024 * 1024

f32 = jnp.float32
bf16 = jnp.bfloat16


def _sigmoid(x):
    return 0.5 * jnp.tanh(0.5 * x) + 0.5


def _silu(x):
    return x * _sigmoid(x)


def _sqrt_unit_interval(x):
    return jnp.where(x == 0.0, 0.0, x * lax.rsqrt(x))


LOG2E = 1.4426950408889634


def _log2_decay(lam):
    y = -lam
    sp = jnp.maximum(y, 0.0) + jnp.log1p(jnp.exp(-jnp.abs(y)))
    return (-C_RG * LOG2E) * sp


def _rms_scale(x):
    return lax.rsqrt(jnp.mean(x * x, axis=-1, keepdims=True) + EPS)


def _mod_kernel(c_ref, w_ref, b_ref, o_ref):
    a = _silu(c_ref[...]).astype(bf16)
    w = w_ref[0].astype(bf16)
    o_ref[0] = jnp.dot(a, w, preferred_element_type=f32) + b_ref[0]


def _modulation(c_all, w_ada, b_ada):
    m = c_all.shape[0]
    return pl.pallas_call(
        _mod_kernel,
        grid=(DEPTH, 3),
        in_specs=[
            pl.BlockSpec((m, D), lambda l, j: (0, 0)),
            pl.BlockSpec((1, D, D), lambda l, j: (l, 0, j)),
            pl.BlockSpec((1, 1, D), lambda l, j: (l, 0, j)),
        ],
        out_specs=pl.BlockSpec((1, m, D), lambda l, j: (l, 0, j)),
        out_shape=jax.ShapeDtypeStruct((DEPTH, m, 3 * D), f32),
        compiler_params=pltpu.CompilerParams(
            dimension_semantics=("arbitrary", "arbitrary")),
        name="adaln_mod",
    )(c_all, w_ada, b_ada.reshape(DEPTH, 1, 3 * D))


def _prompt_layer_kernel(final, first, *refs):
    n_x = 2 * NB if first else 2
    x_refs, refs = refs[:n_x], refs[n_x:]
    (mod_ref, ng_ref, win_ref, vng_ref, ws_ref, bs_ref, cw_ref, cb_ref, wga_ref, wgx_ref,
     brg_ref, lam_ref, wpa_ref, wpb_ref, wout_ref, fg_ref,
     xs_ref, mods_ref, h0_ref, cbuf_ref, ws0_ref, bs0_ref,
     out_ref, hl_ref, cn_ref, xso_ref, hs_ref, cs_ref, vs_ref,
     h_s, bv, bu, bga, bgb, bxc, brp, bip, bpa, slab, vprime, xb_s,
     hst, ya_s, yb_s, xcb_s) = refs[:45]
    stag = refs[45] if (first or final) else None
    i = pl.program_id(0)
    part = i % (CHUNK // TT)
    hrow = pl.multiple_of(part * TT, TT)
    slot = i % 2

    shift = mod_ref[:, 0:D]
    scale1 = 1.0 + mod_ref[:, D:2 * D]
    gate = mod_ref[:, 2 * D:3 * D]
    ng = ng_ref[...]

    def rows_of(j):
        return slice(j * RB, (j + 1) * RB)

    def stage(batch_refs, dst_slot):
        def unit(b):
            def run():
                for c in range(D // GW):
                    stag[dst_slot, c, pl.ds(b, TT, stride=NB), :] = (
                        batch_refs[b][:, c * GW:(c + 1) * GW])
            return run
        return [unit(b) for b in range(NB)]

    def load_x(which, j):
        if not first:
            return x_refs[which][j * 8:(j + 1) * 8]
        s = slot if which == 0 else 1 - slot
        xv = jnp.concatenate([stag[s, c, rows_of(j), :] for c in range(D // GW)], axis=1)
        return xv.reshape(8, NB, D)

    def prenorm(which, dst_slot):
        def unit(j):
            def run():
                xv = load_x(which, j)
                hv = (xv * _rms_scale(xv) * ng) * scale1[None] + shift[None]
                h_s[dst_slot, rows_of(j), :] = hv.reshape(RB, D).astype(bf16)
            return run
        return [unit(j) for j in range(NRB)]

    @pl.when(i == 0)
    def _():
        hst[...] = jnp.zeros_like(hst)
        xb_s[0:TAIL, :] = jnp.zeros((TAIL, D), f32)
        if first:
            for f in stage(x_refs[:NB], 0):
                f()
        for f in prenorm(0, 0):
            f()

    @pl.when(part == 0)
    def _():
        vprime[:, TT:CHUNK, :] = jnp.zeros((NG, CHUNK - TT, NB * GW), bf16)

    def mm(dst, lhs, w_ref, c0=0, r0=0):
        def chunk(n):
            def run():
                a = h_s[slot] if lhs is None else lhs[...]
                dst[r0:r0 + TM, n * GBLK:(n + 1) * GBLK] = jnp.dot(
                    a, w_ref[:, c0 + n * GBLK:c0 + (n + 1) * GBLK],
                    preferred_element_type=f32)
            return run
        return [chunk(n) for n in range(NGB)]

    def proj(dst, c0, r0=0):
        return mm(dst, None, win_ref, c0, r0)

    def interleave(mxu, vpu):
        n, m = len(mxu), len(vpu)
        done = 0
        for k, f in enumerate(mxu):
            upto = (m * (k + 1)) // n
            for g in vpu[done:upto]:
                g()
            done = upto
            f()

    cb = cb_ref[...]
    cw = [cw_ref[k:k + 1, :] for k in range(CONV_W)]

    def p4a(j):
        def run():
            xc = cb
            for k in range(CONV_W):
                xc = xc + cw[k] * xb_s[j * RB + k * NB:(j + 1) * RB + k * NB, :]
            bxc[rows_of(j), :] = xc
            xcb_s[rows_of(j), :] = xc.astype(bf16)
        return run

    def conv_tail():
        tail = xb_s[TM:TM + TAIL, :]
        cn_ref[...] = tail
        xb_s[0:TAIL, :] = tail

    vng = vng_ref[...]

    def p2(j):
        def run():
            vv = bv[rows_of(j), :]
            vn = vv * _rms_scale(vv) * vng
            for g in range(NG):
                slab[g, rows_of(j), :] = vn[:, g * GW:(g + 1) * GW]
        return run

    def relayout(g):
        def run():
            for b in range(NB):
                vprime[g, pl.ds(hrow, TT), b * GW:(b + 1) * GW] = (
                    slab[g, pl.ds(b, TT, stride=NB), :].astype(bf16))
        return run

    def gates(q):
        def run():
            cols = slice(q * GBLK, (q + 1) * GBLK)
            brp[:, cols] = jnp.dot(xcb_s[:, cols], wga_ref[q], preferred_element_type=f32)
            bip[:, cols] = jnp.dot(xcb_s[:, cols], wgx_ref[q], preferred_element_type=f32)
        return run

    t_idx = hrow + lax.broadcasted_iota(jnp.int32, (TT, CHUNK), 0)
    s_idx = lax.broadcasted_iota(jnp.int32, (TT, CHUNK), 1)
    causal = s_idx <= t_idx

    def spatial(g):
        def run():
            wt = jnp.where(causal, ws_ref[g, pl.ds(hrow, TT), :], jnp.zeros((), bf16))
            sp = jnp.dot(wt, vprime[g], preferred_element_type=f32)
            bias = bs_ref[g, pl.ds(hrow, TT), :]
            for b in range(NB):
                slab[g, pl.ds(b, TT, stride=NB), :] = sp[:, b * GW:(b + 1) * GW] + bias
        return run

    def p3(j):
        def run():
            for g in range(NG):
                cols = slice(g * GW, (g + 1) * GW)
                ya_s[rows_of(j), cols] = (
                    bu[rows_of(j), cols] * slab[g, rows_of(j), :]
                    * _silu(bga[rows_of(j), cols])).astype(bf16)
        return run

    ba = brg_ref[0:1, :]
    bx = brg_ref[1:2, :]
    cneg = _log2_decay(lam_ref[...])
    state = [hst[...]]

    def p4b(j):
        def run():
            xc = bxc[rows_of(j), :]
            r = _sigmoid(brp[rows_of(j), :] + ba)
            ig = _sigmoid(bip[rows_of(j), :] + bx)
            a = jnp.exp2(r * cneg)
            xs = _sqrt_unit_interval(1.0 - a * a) * (ig * xc)
            h = state[0]
            ys = []
            for t in range(RB // NB):
                h = a[t * NB:(t + 1) * NB] * h + xs[t * NB:(t + 1) * NB]
                ys.append(h)
            state[0] = h
            yr = jnp.concatenate(ys, axis=0)
            yb_s[rows_of(j), :] = (yr * bgb[rows_of(j), :]).astype(bf16)
        return run

    def gb_act(q):
        def run():
            cols = slice(q * GBLK, (q + 1) * GBLK)
            bgb[:, cols] = _silu(bgb[:, cols])
        return run

    def scan_done():
        hst[...] = state[0]
        hl_ref[...] = state[0]

    def p5z(j):
        def run():
            bv[rows_of(j), :] = _sigmoid(bv[rows_of(j), :])
            bpa[rows_of(j), :] = _sigmoid(bpa[rows_of(j), :])
        return run

    def p5a(j):
        def run():
            bu[rows_of(j), :] = bv[rows_of(j), :] * bu[rows_of(j), :]
        return run

    def p5b(q):
        def run():
            cols = slice(q * GBLK, (q + 1) * GBLK)
            for j in range(NRB):
                xcb_s[rows_of(j), cols] = (
                    bu[rows_of(j), cols] + bpa[rows_of(j), cols] * brp[rows_of(j), cols]
                ).astype(bf16)
        return run

    fg = fg_ref[...]

    def p6(j):
        def run():
            o = bgb[rows_of(j), :].reshape(8, NB, D)
            y = load_x(0, j) + gate[None] * o
            if final:
                y = (y * _rms_scale(y) * fg).reshape(RB, D)
                for c in range(D // GW):
                    stag[c, rows_of(j), :] = y[:, c * GW:(c + 1) * GW]
            else:
                out_ref[j * 8:(j + 1) * 8] = y
        return run

    def unstage(b):
        def run():
            for c in range(D // GW):
                out_ref[b, :, c * GW:(c + 1) * GW] = stag[c, pl.ds(b, TT, stride=NB), :]
        return run

    def units(f):
        return [f(j) for j in range(NRB)]

    interleave(proj(xb_s, C_XB, TAIL), stage(x_refs[NB:], 1 - slot) if first else [])
    interleave(proj(bgb, C_GB), units(p4a) + [conv_tail])
    interleave(proj(bu, C_U), [gb_act(q) for q in range(NGB)])
    interleave([gates(q) for q in range(NGB)], [])
    interleave(proj(bv, C_V), units(p4b)[:NRB // 2])
    interleave(proj(bga, C_GA), units(p4b)[NRB // 2:] + [scan_done])
    interleave(proj(bpa, C_ZB), units(p2) + [relayout(g) for g in range(NG)])
    interleave([spatial(g) for g in range(NG)], [])
    interleave(proj(bv, C_ZA), units(p3))
    interleave(mm(bu, ya_s, wpa_ref), units(p5z) + prenorm(1, 1 - slot))
    interleave(mm(brp, yb_s, wpb_ref), units(p5a))
    interleave([p5b(q) for q in range(NGB)], [])
    interleave(mm(bgb, xcb_s, wout_ref), [])
    for f in units(p6):
        f()
    if final:
        for b in range(NB):
            unstage(b)()

    @pl.when(i == NT - 1)
    def _():
        _sample_layer(final, xs_ref, mods_ref, h0_ref, cbuf_ref, ws0_ref, bs0_ref, ng_ref,
                      win_ref, vng_ref, cw_ref, cb_ref, wga_ref, wgx_ref, brg_ref, lam_ref,
                      wpa_ref, wpb_ref, wout_ref, fg_ref, xso_ref, hs_ref, cs_ref, vs_ref)


def _sample_layer(final, x_ref, mod_ref, h0_ref, cbuf_ref, ws0_ref, bs0_ref, ng_ref, win_ref,
                  vng_ref, cw_ref, cb_ref, wga_ref, wgx_ref, brg_ref, lam_ref, wpa_ref,
                  wpb_ref, wout_ref, fg_ref, xo_ref, hs_ref, cs_ref, vs_ref):
    x = x_ref[...]
    shift = mod_ref[:, 0:D]
    scale = mod_ref[:, D:2 * D]
    gate = mod_ref[:, 2 * D:3 * D]
    h = ((x * _rms_scale(x) * ng_ref[...]) * (1.0 + scale) + shift).astype(bf16)

    def proj(c0):
        return jnp.dot(h, win_ref[:, c0:c0 + D], preferred_element_type=f32)

    v = proj(C_V)
    v = v * _rms_scale(v) * vng_ref[...]
    vs_ref[...] = v
    s = ws0_ref[...] * v + bs0_ref[...]
    ya = (proj(C_U) * s * _silu(proj(C_GA))).astype(bf16)

    xb = proj(C_XB)
    xc = (cb_ref[...] + cw_ref[0:1, :] * cbuf_ref[0] + cw_ref[1:2, :] * cbuf_ref[1]
          + cw_ref[2:3, :] * cbuf_ref[2] + cw_ref[3:4, :] * xb)
    cs_ref[0] = cbuf_ref[1]
    cs_ref[1] = cbuf_ref[2]
    cs_ref[2] = xb
    xcb = xc.astype(bf16)
    rp = jnp.concatenate(
        [jnp.dot(xcb[:, q * GBLK:(q + 1) * GBLK], wga_ref[q], preferred_element_type=f32)
         for q in range(NGB)], axis=1)
    ip = jnp.concatenate(
        [jnp.dot(xcb[:, q * GBLK:(q + 1) * GBLK], wgx_ref[q], preferred_element_type=f32)
         for q in range(NGB)], axis=1)
    r = _sigmoid(rp + brg_ref[0:1, :])
    ig = _sigmoid(ip + brg_ref[1:2, :])
    a = jnp.exp2(r * _log2_decay(lam_ref[...]))
    hn = a * h0_ref[...] + _sqrt_unit_interval(1.0 - a * a) * (ig * xc)
    hs_ref[...] = hn
    yb = (hn * _silu(proj(C_GB))).astype(bf16)

    pa = jnp.dot(ya, wpa_ref[...], preferred_element_type=f32)
    pb = jnp.dot(yb, wpb_ref[...], preferred_element_type=f32)
    mg = (_sigmoid(proj(C_ZA)) * pa + _sigmoid(proj(C_ZB)) * pb).astype(bf16)
    xn = x + gate * jnp.dot(mg, wout_ref[...], preferred_element_type=f32)
    if final:
        xn = xn * _rms_scale(xn) * fg_ref[...]
    xo_ref[...] = xn


def _layer_const_spec(shape, l):
    nd = len(shape)
    return pl.BlockSpec((None,) + tuple(shape[1:]), lambda i: (l,) + (0,) * (nd - 1),
                        pipeline_mode=pl.Buffered(1))


def _layer(l, x_in, x_s, mod, ng, win, vng, ws, bs, cw, cb, wga, wgx, brg, lam,
           wpa, wpb, wout, fg, h0, cbuf, ws0, bs0):
    final = l == DEPTH - 1
    stacked = (ng, win, vng, ws, bs, cw, cb, wga, wgx, brg, lam, wpa, wpb, wout)
    sample_stacked = (h0, cbuf, ws0, bs0)
    xs_spec = pl.BlockSpec((NS, D), lambda i: (0, 0), pipeline_mode=pl.Buffered(1))
    mods_spec = pl.BlockSpec((None, NS, 3 * D), lambda i: (l, 0, 0),
                             pipeline_mode=pl.Buffered(1))
    mod_spec = pl.BlockSpec((None, NB, 3 * D), lambda i: (l, NS // NB, 0),
                            pipeline_mode=pl.Buffered(1))
    fg_spec = pl.BlockSpec((1, D), lambda i: (0, 0), pipeline_mode=pl.Buffered(1))
    first = l == 0
    if first:
        x_specs = ([pl.BlockSpec((None, TT, D), lambda i, b=b: (b, 0, 0),
                                 pipeline_mode=pl.Buffered(1)) for b in range(NB)]
                   + [pl.BlockSpec((None, TT, D),
                                   lambda i, b=b: (b, jnp.minimum(i + 1, NT - 1), 0))
                      for b in range(NB)])
        extra_scratch = [pltpu.VMEM((2, D // GW, TM, GW), f32)]
    else:
        x_specs = [pl.BlockSpec((TT, NB, D), lambda i: (i, 0, 0)),
                   pl.BlockSpec((TT, NB, D), lambda i: (jnp.minimum(i + 1, NT - 1), 0, 0))]
        extra_scratch = []
    if final:
        y_spec = pl.BlockSpec((NB, TT, D), lambda i: (0, i, 0))
        y_shape = jax.ShapeDtypeStruct((NB, SEQ, D), f32)
        extra_scratch = [pltpu.VMEM((D // GW, TM, GW), f32)]
    else:
        y_spec = pl.BlockSpec((TT, NB, D), lambda i: (i, 0, 0))
        y_shape = jax.ShapeDtypeStruct((SEQ, NB, D), f32)
    return pl.pallas_call(
        functools.partial(_prompt_layer_kernel, final, first),
        grid=(NT,),
        in_specs=x_specs
        + [mod_spec] + [_layer_const_spec(a.shape, l) for a in stacked] + [fg_spec]
        + [xs_spec, mods_spec] + [_layer_const_spec(a.shape, l) for a in sample_stacked],
        out_specs=[
            y_spec,
            pl.BlockSpec((NB, D), lambda i: (0, 0)),
            pl.BlockSpec((TAIL, D), lambda i: (0, 0)),
            pl.BlockSpec((NS, D), lambda i: (0, 0)),
            pl.BlockSpec((NS, D), lambda i: (0, 0)),
            pl.BlockSpec((CONV_W - 1, NS, D), lambda i: (0, 0, 0)),
            pl.BlockSpec((NS, D), lambda i: (0, 0)),
        ],
        out_shape=[
            y_shape,
            jax.ShapeDtypeStruct((NB, D), f32),
            jax.ShapeDtypeStruct((TAIL, D), f32),
            jax.ShapeDtypeStruct((NS, D), f32),
            jax.ShapeDtypeStruct((NS, D), f32),
            jax.ShapeDtypeStruct((CONV_W - 1, NS, D), f32),
            jax.ShapeDtypeStruct((NS, D), f32),
        ],
        scratch_shapes=[
            pltpu.VMEM((2, TM, D), bf16),
            pltpu.VMEM((TM, D), f32),
            pltpu.VMEM((TM, D), f32),
            pltpu.VMEM((TM, D), f32),
            pltpu.VMEM((TM, D), f32),
            pltpu.VMEM((TM, D), f32),
            pltpu.VMEM((TM, D), f32),
            pltpu.VMEM((TM, D), f32),
            pltpu.VMEM((TM, D), f32),
            pltpu.VMEM((NG, TM, GW), f32),
            pltpu.VMEM((NG, CHUNK, NB * GW), bf16),
            pltpu.VMEM((TM + TAIL, D), f32),
            pltpu.VMEM((NB, D), f32),
            pltpu.VMEM((TM, D), bf16),
            pltpu.VMEM((TM, D), bf16),
            pltpu.VMEM((TM, D), bf16),
        ] + extra_scratch,
        compiler_params=pltpu.CompilerParams(
            dimension_semantics=("arbitrary",), vmem_limit_bytes=VMEM_LIMIT),
        name="layer_final" if final else ("layer_first" if first else "layer"),
    )(*([x_in] * len(x_specs)), mod, *stacked, fg, x_s, mod, *sample_stacked)


def _block_diag_gate(w):
    per = GBLK // BW
    w = w.reshape(DEPTH, NGB, per, BW, BW)
    eye = jnp.eye(per, dtype=w.dtype)
    out = jnp.einsum('lqhij,hk->lqhikj', w, eye)
    return out.reshape(DEPTH, NGB, GBLK, GBLK)


def kernel(x_prompt, x_sample, c_prompt, c_sample, state_rglru_h, state_conv, w_ada, b_ada,
           norm_g, w_in, v_norm_g, w_s, b_s, conv_w, conv_b, w_rg_a, b_rg_a, w_rg_x,
           b_rg_x, lam, w_pa, w_pb, w_out, final_g):
    c_all = jnp.concatenate([c_sample, c_prompt, jnp.zeros((8, D), f32)], axis=0)
    mod = _modulation(c_all, w_ada, b_ada)

    win = w_in.astype(bf16)
    wpa = w_pa.astype(bf16)
    wpb = w_pb.astype(bf16)
    wout = w_out.astype(bf16)
    wga = _block_diag_gate(w_rg_a).astype(bf16)
    wgx = _block_diag_gate(w_rg_x).astype(bf16)
    ws = w_s.astype(bf16)
    bs = jnp.broadcast_to(b_s[..., None], (DEPTH, NG, CHUNK, GW))
    ng = norm_g.reshape(DEPTH, 1, D)
    vng = v_norm_g.reshape(DEPTH, 1, D)
    cb = conv_b.reshape(DEPTH, 1, D)
    brg = jnp.stack([b_rg_a, b_rg_x], axis=1)
    lam3 = lam.reshape(DEPTH, 1, D)
    fg = final_g.reshape(1, D)

    ws0 = jnp.repeat(w_s[:, :, 0, 0], GW, axis=-1).reshape(DEPTH, 1, D)
    bs0 = jnp.repeat(b_s[:, :, 0], GW, axis=-1).reshape(DEPTH, 1, D)
    cbuf = state_conv.transpose(0, 2, 1, 3)

    x_p = x_prompt
    x_s = x_sample.reshape(NS, D)
    hp, cp, hs, cs, vs = [], [], [], [], []
    for l in range(DEPTH):
        x_p, h_l, c_l, x_s, hs_l, cs_l, vs_l = _layer(
            l, x_p, x_s, mod, ng, win, vng, ws, bs, conv_w, cb, wga, wgx, brg, lam3,
            wpa, wpb, wout, fg, state_rglru_h, cbuf, ws0, bs0)
        hp.append(h_l)
        cp.append(c_l.reshape(CONV_W - 1, NB, D).transpose(1, 0, 2))
        hs.append(hs_l)
        cs.append(cs_l.transpose(1, 0, 2))
        vs.append(vs_l)
    y_prompt = x_p
    h_prompt = jnp.stack(hp)
    conv_prompt = jnp.stack(cp)
    y_sample = x_s.reshape(NS, 1, D)
    h_sample = jnp.stack(hs)
    conv_sample = jnp.stack(cs)
    chunk_v_sample = jnp.stack(vs).reshape(DEPTH, NS, 1, D)
    return (y_prompt, y_sample, h_prompt, conv_prompt, h_sample, conv_sample,
            chunk_v_sample)
```

```python
import functools

import jax
import jax.numpy as jnp
from jax import lax
from jax.experimental import pallas as pl
from jax.experimental.pallas import tpu as pltpu

D = 1024
NB = 8
SEQ = 2048
DEPTH = 4
NS = 128
CHUNK = 128
GW = 128
NG = D // GW
HB = 16
BW = D // HB
CONV_W = 4
C_RG = 8.0
EPS = 1e-6
D_IN = 7 * D
C_U, C_V, C_GA, C_XB, C_GB, C_ZA, C_ZB = (i * D for i in range(7))

TT = 32
TM = TT * NB
NT = SEQ // TT
RB = 64
NRB = TM // RB
GBLK = 256
NGB = D // GBLK
TAIL = (CONV_W - 1) * NB

VMEM_LIMIT = 62 * 1024 * 1024

f32 = jnp.float32
bf16 = jnp.bfloat16


def _sigmoid(x):
    return 0.5 * jnp.tanh(0.5 * x) + 0.5


def _silu(x):
    return x * _sigmoid(x)


def _sqrt_unit_interval(x):
    return jnp.where(x == 0.0, 0.0, x * lax.rsqrt(x))


LOG2E = 1.4426950408889634


def _log2_decay(lam):
    y = -lam
    sp = jnp.maximum(y, 0.0) + jnp.log1p(jnp.exp(-jnp.abs(y)))
    return (-C_RG * LOG2E) * sp


def _rms_scale(x):
    return lax.rsqrt(jnp.mean(x * x, axis=-1, keepdims=True) + EPS)


def _mod_kernel(c_ref, w_ref, b_ref, o_ref):
    a = _silu(c_ref[...]).astype(bf16)
    w = w_ref[0].astype(bf16)
    o_ref[0] = jnp.dot(a, w, preferred_element_type=f32) + b_ref[0]


def _modulation(c_all, w_ada, b_ada):
    m = c_all.shape[0]
    return pl.pallas_call(
        _mod_kernel,
        grid=(DEPTH, 3),
        in_specs=[
            pl.BlockSpec((m, D), lambda l, j: (0, 0)),
            pl.BlockSpec((1, D, D), lambda l, j: (l, 0, j)),
            pl.BlockSpec((1, 1, D), lambda l, j: (l, 0, j)),
        ],
        out_specs=pl.BlockSpec((1, m, D), lambda l, j: (l, 0, j)),
        out_shape=jax.ShapeDtypeStruct((DEPTH, m, 3 * D), f32),
        compiler_params=pltpu.CompilerParams(
            dimension_semantics=("arbitrary", "arbitrary")),
        name="adaln_mod",
    )(c_all, w_ada, b_ada.reshape(DEPTH, 1, 3 * D))


def _prompt_layer_kernel(final, first, *refs):
    n_x = 2 * NB if first else 2
    x_refs, refs = refs[:n_x], refs[n_x:]
    (mod_ref, ng_ref, win_ref, vng_ref, ws_ref, bs_ref, cw_ref, cb_ref, wga_ref, wgx_ref,
     brg_ref, lam_ref, wpa_ref, wpb_ref, wout_ref, fg_ref,
     xs_ref, mods_ref, h0_ref, cbuf_ref, ws0_ref, bs0_ref) = refs[:22]
    refs = refs[22:]
    next_f32, refs = ((), refs) if final else (refs[:4], refs[4:])
    out_ref, hl_ref, cn_ref, xso_ref, hs_ref, cs_ref, vs_ref = refs[:7]
    refs = refs[7:]
    next_bf16, refs = ((), refs) if final else (refs[:4], refs[4:])
    (h_s, bv, bu, bga, bgb, bxc, brp, bip, bpa, slab, vprime, xb_s,
     hst, ya_s, yb_s, xcb_s) = refs[:16]
    stag = refs[16] if (first or final) else None
    i = pl.program_id(0)
    part = i % (CHUNK // TT)
    hrow = pl.multiple_of(part * TT, TT)
    slot = i % 2

    shift = mod_ref[:, 0:D]
    scale1 = 1.0 + mod_ref[:, D:2 * D]
    gate = mod_ref[:, 2 * D:3 * D]
    ng = ng_ref[...]

    def rows_of(j):
        return slice(j * RB, (j + 1) * RB)

    def stage(batch_refs, dst_slot):
        def unit(b):
            def run():
                for c in range(D // GW):
                    stag[dst_slot, c, pl.ds(b, TT, stride=NB), :] = (
                        batch_refs[b][:, c * GW:(c + 1) * GW])
            return run
        return [unit(b) for b in range(NB)]

    def load_x(which, j):
        if not first:
            return x_refs[which][j * 8:(j + 1) * 8]
        s = slot if which == 0 else 1 - slot
        xv = jnp.concatenate([stag[s, c, rows_of(j), :] for c in range(D // GW)], axis=1)
        return xv.reshape(8, NB, D)

    def prenorm(which, dst_slot):
        def unit(j):
            def run():
                xv = load_x(which, j)
                hv = (xv * _rms_scale(xv) * ng) * scale1[None] + shift[None]
                h_s[dst_slot, rows_of(j), :] = hv.reshape(RB, D).astype(bf16)
            return run
        return [unit(j) for j in range(NRB)]

    @pl.when(i == 0)
    def _():
        hst[...] = jnp.zeros_like(hst)
        xb_s[0:TAIL, :] = jnp.zeros((TAIL, D), f32)
        if first:
            for f in stage(x_refs[:NB], 0):
                f()
        for f in prenorm(0, 0):
            f()

    @pl.when(part == 0)
    def _():
        vprime[:, TT:CHUNK, :] = jnp.zeros((NG, CHUNK - TT, NB * GW), bf16)

    def mm(dst, lhs, w_ref, c0=0, r0=0):
        def chunk(n):
            def run():
                a = h_s[slot] if lhs is None else lhs[...]
                dst[r0:r0 + TM, n * GBLK:(n + 1) * GBLK] = jnp.dot(
                    a, w_ref[:, c0 + n * GBLK:c0 + (n + 1) * GBLK],
                    preferred_element_type=f32)
            return run
        return [chunk(n) for n in range(NGB)]

    def proj(dst, c0, r0=0):
        return mm(dst, None, win_ref, c0, r0)

    def interleave(mxu, vpu):
        n, m = len(mxu), len(vpu)
        done = 0
        for k, f in enumerate(mxu):
            upto = (m * (k + 1)) // n
            for g in vpu[done:upto]:
                g()
            done = upto
            f()

    cb = cb_ref[...]
    cw = [cw_ref[k:k + 1, :] for k in range(CONV_W)]

    def p4a(j):
        def run():
            xc = cb
            for k in range(CONV_W):
                xc = xc + cw[k] * xb_s[j * RB + k * NB:(j + 1) * RB + k * NB, :]
            bxc[rows_of(j), :] = xc
            xcb_s[rows_of(j), :] = xc.astype(bf16)
        return run

    def conv_tail():
        tail = xb_s[TM:TM + TAIL, :]
        cn_ref[...] = tail
        xb_s[0:TAIL, :] = tail

    vng = vng_ref[...]

    def p2(j):
        def run():
            vv = bv[rows_of(j), :]
            vn = vv * _rms_scale(vv) * vng
            for g in range(NG):
                slab[g, rows_of(j), :] = vn[:, g * GW:(g + 1) * GW]
        return run

    def relayout(g):
        def run():
            for b in range(NB):
                vprime[g, pl.ds(hrow, TT), b * GW:(b + 1) * GW] = (
                    slab[g, pl.ds(b, TT, stride=NB), :].astype(bf16))
        return run

    def gates(q):
        def run():
            cols = slice(q * GBLK, (q + 1) * GBLK)
            brp[:, cols] = jnp.dot(xcb_s[:, cols], wga_ref[q], preferred_element_type=f32)
            bip[:, cols] = jnp.dot(xcb_s[:, cols], wgx_ref[q], preferred_element_type=f32)
        return run

    t_idx = hrow + lax.broadcasted_iota(jnp.int32, (TT, CHUNK), 0)
    s_idx = lax.broadcasted_iota(jnp.int32, (TT, CHUNK), 1)
    causal = s_idx <= t_idx

    def spatial(g):
        def run():
            wt = jnp.where(causal, ws_ref[g, pl.ds(hrow, TT), :], jnp.zeros((), bf16))
            sp = jnp.dot(wt, vprime[g], preferred_element_type=f32)
            bias = bs_ref[g, pl.ds(hrow, TT), :]
            for b in range(NB):
                slab[g, pl.ds(b, TT, stride=NB), :] = sp[:, b * GW:(b + 1) * GW] + bias
        return run

    def p3(j):
        def run():
            for g in range(NG):
                cols = slice(g * GW, (g + 1) * GW)
                ya_s[rows_of(j), cols] = (
                    bu[rows_of(j), cols] * slab[g, rows_of(j), :]
                    * _silu(bga[rows_of(j), cols])).astype(bf16)
        return run

    ba = brg_ref[0:1, :]
    bx = brg_ref[1:2, :]
    cneg = _log2_decay(lam_ref[...])
    state = [hst[...]]

    def p4b(j):
        def run():
            xc = bxc[rows_of(j), :]
            r = _sigmoid(brp[rows_of(j), :] + ba)
            ig = _sigmoid(bip[rows_of(j), :] + bx)
            a = jnp.exp2(r * cneg)
            xs = _sqrt_unit_interval(1.0 - a * a) * (ig * xc)
            h = state[0]
            ys = []
            for t in range(RB // NB):
                h = a[t * NB:(t + 1) * NB] * h + xs[t * NB:(t + 1) * NB]
                ys.append(h)
            state[0] = h
            yr = jnp.concatenate(ys, axis=0)
            yb_s[rows_of(j), :] = (yr * bgb[rows_of(j), :]).astype(bf16)
        return run

    def gb_act(q):
        def run():
            cols = slice(q * GBLK, (q + 1) * GBLK)
            bgb[:, cols] = _silu(bgb[:, cols])
        return run

    def scan_done():
        hst[...] = state[0]
        hl_ref[...] = state[0]

    def p5z(j):
        def run():
            bv[rows_of(j), :] = _sigmoid(bv[rows_of(j), :])
            bpa[rows_of(j), :] = _sigmoid(bpa[rows_of(j), :])
        return run

    def p5a(j):
        def run():
            bu[rows_of(j), :] = bv[rows_of(j), :] * bu[rows_of(j), :]
        return run

    def p5b(q):
        def run():
            cols = slice(q * GBLK, (q + 1) * GBLK)
            for j in range(NRB):
                xcb_s[rows_of(j), cols] = (
                    bu[rows_of(j), cols] + bpa[rows_of(j), cols] * brp[rows_of(j), cols]
                ).astype(bf16)
        return run

    fg = fg_ref[...]

    def p6(j):
        def run():
            o = bgb[rows_of(j), :].reshape(8, NB, D)
            y = load_x(0, j) + gate[None] * o
            if final:
                y = (y * _rms_scale(y) * fg).reshape(RB, D)
                for c in range(D // GW):
                    stag[c, rows_of(j), :] = y[:, c * GW:(c + 1) * GW]
            else:
                out_ref[j * 8:(j + 1) * 8] = y
        return run

    def unstage(b):
        def run():
            for c in range(D // GW):
                out_ref[b, :, c * GW:(c + 1) * GW] = stag[c, pl.ds(b, TT, stride=NB), :]
        return run

    def units(f):
        return [f(j) for j in range(NRB)]

    for src, dst in zip(next_f32, next_bf16):
        dst[...] = src[...].astype(bf16)

    interleave(proj(xb_s, C_XB, TAIL), stage(x_refs[NB:], 1 - slot) if first else [])
    interleave(proj(bgb, C_GB), units(p4a) + [conv_tail])
    interleave(proj(bu, C_U), [gb_act(q) for q in range(NGB)])
    interleave([gates(q) for q in range(NGB)], [])
    interleave(proj(bv, C_V), units(p4b)[:NRB // 2])
    interleave(proj(bga, C_GA), units(p4b)[NRB // 2:] + [scan_done])
    interleave(proj(bpa, C_ZB), units(p2) + [relayout(g) for g in range(NG)])
    interleave([spatial(g) for g in range(NG)], [])
    interleave(proj(bv, C_ZA), units(p3))
    interleave(mm(bu, ya_s, wpa_ref), units(p5z) + prenorm(1, 1 - slot))
    interleave(mm(brp, yb_s, wpb_ref), units(p5a))
    interleave([p5b(q) for q in range(NGB)], [])
    interleave(mm(bgb, xcb_s, wout_ref), [])
    for f in units(p6):
        f()
    if final:
        for b in range(NB):
            unstage(b)()

    @pl.when(i == NT - 1)
    def _():
        _sample_layer(final, xs_ref, mods_ref, h0_ref, cbuf_ref, ws0_ref, bs0_ref, ng_ref,
                      win_ref, vng_ref, cw_ref, cb_ref, wga_ref, wgx_ref, brg_ref, lam_ref,
                      wpa_ref, wpb_ref, wout_ref, fg_ref, xso_ref, hs_ref, cs_ref, vs_ref)


def _sample_layer(final, x_ref, mod_ref, h0_ref, cbuf_ref, ws0_ref, bs0_ref, ng_ref, win_ref,
                  vng_ref, cw_ref, cb_ref, wga_ref, wgx_ref, brg_ref, lam_ref, wpa_ref,
                  wpb_ref, wout_ref, fg_ref, xo_ref, hs_ref, cs_ref, vs_ref):
    x = x_ref[...]
    shift = mod_ref[:, 0:D]
    scale = mod_ref[:, D:2 * D]
    gate = mod_ref[:, 2 * D:3 * D]
    h = ((x * _rms_scale(x) * ng_ref[...]) * (1.0 + scale) + shift).astype(bf16)

    def proj(c0):
        return jnp.dot(h, win_ref[:, c0:c0 + D], preferred_element_type=f32)

    v = proj(C_V)
    v = v * _rms_scale(v) * vng_ref[...]
    vs_ref[...] = v
    s = ws0_ref[...] * v + bs0_ref[...]
    ya = (proj(C_U) * s * _silu(proj(C_GA))).astype(bf16)

    xb = proj(C_XB)
    xc = (cb_ref[...] + cw_ref[0:1, :] * cbuf_ref[0] + cw_ref[1:2, :] * cbuf_ref[1]
          + cw_ref[2:3, :] * cbuf_ref[2] + cw_ref[3:4, :] * xb)
    cs_ref[0] = cbuf_ref[1]
    cs_ref[1] = cbuf_ref[2]
    cs_ref[2] = xb
    xcb = xc.astype(bf16)
    rp = jnp.concatenate(
        [jnp.dot(xcb[:, q * GBLK:(q + 1) * GBLK], wga_ref[q], preferred_element_type=f32)
         for q in range(NGB)], axis=1)
    ip = jnp.concatenate(
        [jnp.dot(xcb[:, q * GBLK:(q + 1) * GBLK], wgx_ref[q], preferred_element_type=f32)
         for q in range(NGB)], axis=1)
    r = _sigmoid(rp + brg_ref[0:1, :])
    ig = _sigmoid(ip + brg_ref[1:2, :])
    a = jnp.exp2(r * _log2_decay(lam_ref[...]))
    hn = a * h0_ref[...] + _sqrt_unit_interval(1.0 - a * a) * (ig * xc)
    hs_ref[...] = hn
    yb = (hn * _silu(proj(C_GB))).astype(bf16)

    pa = jnp.dot(ya, wpa_ref[...], preferred_element_type=f32)
    pb = jnp.dot(yb, wpb_ref[...], preferred_element_type=f32)
    mg = (_sigmoid(proj(C_ZA)) * pa + _sigmoid(proj(C_ZB)) * pb).astype(bf16)
    xn = x + gate * jnp.dot(mg, wout_ref[...], preferred_element_type=f32)
    if final:
        xn = xn * _rms_scale(xn) * fg_ref[...]
    xo_ref[...] = xn


def _layer_const_spec(shape, l):
    nd = len(shape)
    return pl.BlockSpec((None,) + tuple(shape[1:]), lambda i: (l,) + (0,) * (nd - 1),
                        pipeline_mode=pl.Buffered(1))


def _layer(l, x_in, x_s, mod, ng, win, vng, ws, bs, cw, cb, wga, wgx, brg, lam,
           wpa, wpb, wout, fg, h0, cbuf, ws0, bs0, next_f32):
    final = l == DEPTH - 1
    own = {id(win), id(wpa), id(wpb), id(wout)}
    consts = (ng, win, vng, ws, bs, cw, cb, wga, wgx, brg, lam, wpa, wpb, wout)
    const_specs = [
        pl.BlockSpec(a.shape, lambda i: (0, 0), pipeline_mode=pl.Buffered(1))
        if id(a) in own else _layer_const_spec(a.shape, l) for a in consts]
    sample_stacked = (h0, cbuf, ws0, bs0)
    rows = D // NT
    if final:
        next_f32, cast_in_specs, cast_out_specs, cast_shapes = (), [], [], []
    else:
        cast_in_specs = [pl.BlockSpec((None, rows, w.shape[2]), lambda i: (l + 1, i, 0))
                         for w in next_f32]
        cast_out_specs = [pl.BlockSpec((rows, w.shape[2]), lambda i: (i, 0))
                          for w in next_f32]
        cast_shapes = [jax.ShapeDtypeStruct(w.shape[1:], bf16) for w in next_f32]
    xs_spec = pl.BlockSpec((NS, D), lambda i: (0, 0), pipeline_mode=pl.Buffered(1))
    mods_spec = pl.BlockSpec((None, NS, 3 * D), lambda i: (l, 0, 0),
                             pipeline_mode=pl.Buffered(1))
    mod_spec = pl.BlockSpec((None, NB, 3 * D), lambda i: (l, NS // NB, 0),
                            pipeline_mode=pl.Buffered(1))
    fg_spec = pl.BlockSpec((1, D), lambda i: (0, 0), pipeline_mode=pl.Buffered(1))
    first = l == 0
    if first:
        x_specs = ([pl.BlockSpec((None, TT, D), lambda i, b=b: (b, 0, 0),
                                 pipeline_mode=pl.Buffered(1)) for b in range(NB)]
                   + [pl.BlockSpec((None, TT, D),
                                   lambda i, b=b: (b, jnp.minimum(i + 1, NT - 1), 0))
                      for b in range(NB)])
        extra_scratch = [pltpu.VMEM((2, D // GW, TM, GW), f32)]
    else:
        x_specs = [pl.BlockSpec((TT, NB, D), lambda i: (i, 0, 0)),
                   pl.BlockSpec((TT, NB, D), lambda i: (jnp.minimum(i + 1, NT - 1), 0, 0))]
        extra_scratch = []
    if final:
        y_spec = pl.BlockSpec((NB, TT, D), lambda i: (0, i, 0))
        y_shape = jax.ShapeDtypeStruct((NB, SEQ, D), f32)
        extra_scratch = [pltpu.VMEM((D // GW, TM, GW), f32)]
    else:
        y_spec = pl.BlockSpec((TT, NB, D), lambda i: (i, 0, 0))
        y_shape = jax.ShapeDtypeStruct((SEQ, NB, D), f32)
    return pl.pallas_call(
        functools.partial(_prompt_layer_kernel, final, first),
        grid=(NT,),
        in_specs=x_specs
        + [mod_spec] + const_specs + [fg_spec]
        + [xs_spec, mods_spec] + [_layer_const_spec(a.shape, l) for a in sample_stacked]
        + cast_in_specs,
        out_specs=[
            y_spec,
            pl.BlockSpec((NB, D), lambda i: (0, 0)),
            pl.BlockSpec((TAIL, D), lambda i: (0, 0)),
            pl.BlockSpec((NS, D), lambda i: (0, 0)),
            pl.BlockSpec((NS, D), lambda i: (0, 0)),
            pl.BlockSpec((CONV_W - 1, NS, D), lambda i: (0, 0, 0)),
            pl.BlockSpec((NS, D), lambda i: (0, 0)),
        ] + cast_out_specs,
        out_shape=[
            y_shape,
            jax.ShapeDtypeStruct((NB, D), f32),
            jax.ShapeDtypeStruct((TAIL, D), f32),
            jax.ShapeDtypeStruct((NS, D), f32),
            jax.ShapeDtypeStruct((NS, D), f32),
            jax.ShapeDtypeStruct((CONV_W - 1, NS, D), f32),
            jax.ShapeDtypeStruct((NS, D), f32),
        ] + cast_shapes,
        scratch_shapes=[
            pltpu.VMEM((2, TM, D), bf16),
            pltpu.VMEM((TM, D), f32),
            pltpu.VMEM((TM, D), f32),
            pltpu.VMEM((TM, D), f32),
            pltpu.VMEM((TM, D), f32),
            pltpu.VMEM((TM, D), f32),
            pltpu.VMEM((TM, D), f32),
            pltpu.VMEM((TM, D), f32),
            pltpu.VMEM((TM, D), f32),
            pltpu.VMEM((NG, TM, GW), f32),
            pltpu.VMEM((NG, CHUNK, NB * GW), bf16),
            pltpu.VMEM((TM + TAIL, D), f32),
            pltpu.VMEM((NB, D), f32),
            pltpu.VMEM((TM, D), bf16),
            pltpu.VMEM((TM, D), bf16),
            pltpu.VMEM((TM, D), bf16),
        ] + extra_scratch,
        compiler_params=pltpu.CompilerParams(
            dimension_semantics=("arbitrary",), vmem_limit_bytes=VMEM_LIMIT),
        name="layer_final" if final else ("layer_first" if first else "layer"),
    )(*([x_in] * len(x_specs)), mod, *consts, fg, x_s, mod, *sample_stacked, *next_f32)


def _block_diag_gate(w):
    per = GBLK // BW
    w = w.reshape(DEPTH, NGB, per, BW, BW)
    eye = jnp.eye(per, dtype=w.dtype)
    out = jnp.einsum('lqhij,hk->lqhikj', w, eye)
    return out.reshape(DEPTH, NGB, GBLK, GBLK)


def kernel(x_prompt, x_sample, c_prompt, c_sample, state_rglru_h, state_conv, w_ada, b_ada,
           norm_g, w_in, v_norm_g, w_s, b_s, conv_w, conv_b, w_rg_a, b_rg_a, w_rg_x,
           b_rg_x, lam, w_pa, w_pb, w_out, final_g):
    c_all = jnp.concatenate([c_sample, c_prompt, jnp.zeros((8, D), f32)], axis=0)
    mod = _modulation(c_all, w_ada, b_ada)

    big_f32 = (w_in, w_pa, w_pb, w_out)
    win, wpa, wpb, wout = (w[0].astype(bf16) for w in big_f32)
    wga = _block_diag_gate(w_rg_a).astype(bf16)
    wgx = _block_diag_gate(w_rg_x).astype(bf16)
    ws = w_s.astype(bf16)
    bs = jnp.broadcast_to(b_s[..., None], (DEPTH, NG, CHUNK, GW))
    ng = norm_g.reshape(DEPTH, 1, D)
    vng = v_norm_g.reshape(DEPTH, 1, D)
    cb = conv_b.reshape(DEPTH, 1, D)
    brg = jnp.stack([b_rg_a, b_rg_x], axis=1)
    lam3 = lam.reshape(DEPTH, 1, D)
    fg = final_g.reshape(1, D)

    ws0 = jnp.repeat(w_s[:, :, 0, 0], GW, axis=-1).reshape(DEPTH, 1, D)
    bs0 = jnp.repeat(b_s[:, :, 0], GW, axis=-1).reshape(DEPTH, 1, D)
    cbuf = state_conv.transpose(0, 2, 1, 3)

    x_p = x_prompt
    x_s = x_sample.reshape(NS, D)
    hp, cp, hs, cs, vs = [], [], [], [], []
    for l in range(DEPTH):
        x_p, h_l, c_l, x_s, hs_l, cs_l, vs_l, *next_bf16 = _layer(
            l, x_p, x_s, mod, ng, win, vng, ws, bs, conv_w, cb, wga, wgx, brg, lam3,
            wpa, wpb, wout, fg, state_rglru_h, cbuf, ws0, bs0, big_f32)
        if next_bf16:
            win, wpa, wpb, wout = next_bf16
        hp.append(h_l)
        cp.append(c_l.reshape(CONV_W - 1, NB, D).transpose(1, 0, 2))
        hs.append(hs_l)
        cs.append(cs_l.transpose(1, 0, 2))
        vs.append(vs_l)
    y_prompt = x_p
    h_prompt = jnp.stack(hp)
    conv_prompt = jnp.stack(cp)
    y_sample = x_s.reshape(NS, 1, D)
    h_sample = jnp.stack(hs)
    conv_sample = jnp.stack(cs)
    chunk_v_sample = jnp.stack(vs).reshape(DEPTH, NS, 1, D)
    return (y_prompt, y_sample, h_prompt, conv_prompt, h_sample, conv_sample,
            chunk_v_sample)
```

```python
import functools

import jax
import jax.numpy as jnp
from jax import lax
from jax.experimental import pallas as pl
from jax.experimental.pallas import tpu as pltpu

D = 1024
NB = 8
SEQ = 2048
DEPTH = 4
NS = 128
CHUNK = 128
GW = 128
NG = D // GW
HB = 16
BW = D // HB
CONV_W = 4
C_RG = 8.0
EPS = 1e-6
D_IN = 7 * D
C_U, C_V, C_GA, C_XB, C_GB, C_ZA, C_ZB = (i * D for i in range(7))

TT = 32
TM = TT * NB
NT = SEQ // TT
RB = 64
NRB = TM // RB
GBLK = 256
NGB = D // GBLK
TAIL = (CONV_W - 1) * NB

VMEM_LIMIT = 62 * 1024 * 1024

f32 = jnp.float32
bf16 = jnp.bfloat16


def _sigmoid(x):
    return 0.5 * jnp.tanh(0.5 * x) + 0.5


def _silu(x):
    return x * _sigmoid(x)


def _sqrt_unit_interval(x):
    return jnp.where(x == 0.0, 0.0, x * lax.rsqrt(x))


LOG2E = 1.4426950408889634


def _log2_decay(lam):
    y = -lam
    sp = jnp.maximum(y, 0.0) + jnp.log1p(jnp.exp(-jnp.abs(y)))
    return (-C_RG * LOG2E) * sp


def _rms_scale(x):
    return lax.rsqrt(jnp.mean(x * x, axis=-1, keepdims=True) + EPS)


def _mod_kernel(c_ref, w_ref, b_ref, o_ref):
    a = _silu(c_ref[...]).astype(bf16)
    w = w_ref[0].astype(bf16)
    o_ref[0] = jnp.dot(a, w, preferred_element_type=f32) + b_ref[0]


def _modulation(c_all, w_ada, b_ada):
    m = c_all.shape[0]
    return pl.pallas_call(
        _mod_kernel,
        grid=(DEPTH, 3),
        in_specs=[
            pl.BlockSpec((m, D), lambda l, j: (0, 0)),
            pl.BlockSpec((1, D, D), lambda l, j: (l, 0, j)),
            pl.BlockSpec((1, 1, D), lambda l, j: (l, 0, j)),
        ],
        out_specs=pl.BlockSpec((1, m, D), lambda l, j: (l, 0, j)),
        out_shape=jax.ShapeDtypeStruct((DEPTH, m, 3 * D), f32),
        compiler_params=pltpu.CompilerParams(
            dimension_semantics=("arbitrary", "arbitrary")),
        name="adaln_mod",
    )(c_all, w_ada, b_ada.reshape(DEPTH, 1, 3 * D))


def _prompt_layer_kernel(final, first, *refs):
    n_x = 2 * NB if first else 2
    x_refs, refs = refs[:n_x], refs[n_x:]
    (mod_ref, ng_ref, win_ref, vng_ref, ws_ref, bs_ref, cw_ref, cb_ref, wga_ref, wgx_ref,
     brg_ref, lam_ref, wpa_ref, wpb_ref, wout_ref, fg_ref,
     xs_ref, mods_ref, h0_ref, cbuf_ref, ws0_ref, bs0_ref) = refs[:22]
    refs = refs[22:]
    next_f32, refs = ((), refs) if final else (refs[:4], refs[4:])
    out_ref, hl_ref, cn_ref, xso_ref, hs_ref, cs_ref, vs_ref = refs[:7]
    refs = refs[7:]
    next_bf16, refs = ((), refs) if final else (refs[:4], refs[4:])
    (h_s, bv, bu, bga, bgb, bxc, brp, bip, bpa, slab, vprime, xb_s,
     hst, ya_s, yb_s, xcb_s) = refs[:16]
    stag = refs[16] if (first or final) else None
    i = pl.program_id(0)
    part = i % (CHUNK // TT)
    hrow = pl.multiple_of(part * TT, TT)
    slot = i % 2

    shift = mod_ref[:, 0:D]
    scale1 = 1.0 + mod_ref[:, D:2 * D]
    gate = mod_ref[:, 2 * D:3 * D]
    ng = ng_ref[...]

    def rows_of(j):
        return slice(j * RB, (j + 1) * RB)

    def stage(batch_refs, dst_slot):
        def unit(b):
            def run():
                for c in range(D // GW):
                    stag[dst_slot, c, pl.ds(b, TT, stride=NB), :] = (
                        batch_refs[b][:, c * GW:(c + 1) * GW])
            return run
        return [unit(b) for b in range(NB)]

    def load_x(which, j):
        if not first:
            return x_refs[which][j * 8:(j + 1) * 8]
        s = slot if which == 0 else 1 - slot
        xv = jnp.concatenate([stag[s, c, rows_of(j), :] for c in range(D // GW)], axis=1)
        return xv.reshape(8, NB, D)

    anchors = []

    def prenorm(which, dst_slot):
        def unit(j):
            def run():
                xv = load_x(which, j)
                hv = ((xv * _rms_scale(xv) * ng) * scale1[None] + shift[None]).reshape(RB, D)
                h_s[dst_slot, rows_of(j), :] = hv.astype(bf16)
                if which == 1:
                    anchors.append(hv[0:8, 0:GW])
            return run
        return [unit(j) for j in range(NRB)]

    @pl.when(i == 0)
    def _():
        hst[...] = jnp.zeros_like(hst)
        xb_s[0:TAIL, :] = jnp.zeros((TAIL, D), f32)
        if first:
            for f in stage(x_refs[:NB], 0):
                f()
        for f in prenorm(0, 0):
            f()

    @pl.when(part == 0)
    def _():
        vprime[:, TT:CHUNK, :] = jnp.zeros((NG, CHUNK - TT, NB * GW), bf16)

    def mm(dst, lhs, w_ref, c0=0, r0=0):
        def chunk(n):
            def run():
                a = h_s[slot] if lhs is None else lhs[...]
                dst[r0:r0 + TM, n * GBLK:(n + 1) * GBLK] = jnp.dot(
                    a, w_ref[:, c0 + n * GBLK:c0 + (n + 1) * GBLK],
                    preferred_element_type=f32)
            return run
        return [chunk(n) for n in range(NGB)]

    def proj(dst, c0, r0=0):
        return mm(dst, None, win_ref, c0, r0)

    def interleave(mxu, vpu):
        n, m = len(mxu), len(vpu)
        done = 0
        for k, f in enumerate(mxu):
            upto = (m * (k + 1)) // n
            for g in vpu[done:upto]:
                g()
            done = upto
            f()

    cb = cb_ref[...]
    cw = [cw_ref[k:k + 1, :] for k in range(CONV_W)]

    def p4a(j):
        def run():
            xc = cb
            for k in range(CONV_W):
                xc = xc + cw[k] * xb_s[j * RB + k * NB:(j + 1) * RB + k * NB, :]
            bxc[rows_of(j), :] = xc
            xcb_s[rows_of(j), :] = xc.astype(bf16)
        return run

    def conv_tail():
        tail = xb_s[TM:TM + TAIL, :]
        cn_ref[...] = tail
        xb_s[0:TAIL, :] = tail

    vng = vng_ref[...]

    def p2(j):
        def run():
            vv = bv[rows_of(j), :]
            vn = vv * _rms_scale(vv) * vng
            for g in range(NG):
                slab[g, rows_of(j), :] = vn[:, g * GW:(g + 1) * GW]
        return run

    def relayout(g):
        def run():
            for b in range(NB):
                vprime[g, pl.ds(hrow, TT), b * GW:(b + 1) * GW] = (
                    slab[g, pl.ds(b, TT, stride=NB), :].astype(bf16))
        return run

    def gates(q):
        def run():
            cols = slice(q * GBLK, (q + 1) * GBLK)
            brp[:, cols] = jnp.dot(xcb_s[:, cols], wga_ref[q], preferred_element_type=f32)
            bip[:, cols] = jnp.dot(xcb_s[:, cols], wgx_ref[q], preferred_element_type=f32)
        return run

    t_idx = hrow + lax.broadcasted_iota(jnp.int32, (TT, CHUNK), 0)
    s_idx = lax.broadcasted_iota(jnp.int32, (TT, CHUNK), 1)
    causal = s_idx <= t_idx

    def spatial(g):
        def run():
            wt = jnp.where(causal, ws_ref[g, pl.ds(hrow, TT), :], jnp.zeros((), bf16))
            sp = jnp.dot(wt, vprime[g], preferred_element_type=f32)
            bias = bs_ref[g, pl.ds(hrow, TT), :]
            for b in range(NB):
                slab[g, pl.ds(b, TT, stride=NB), :] = sp[:, b * GW:(b + 1) * GW] + bias
        return run

    def p3(j):
        def run():
            for g in range(NG):
                cols = slice(g * GW, (g + 1) * GW)
                ya_s[rows_of(j), cols] = (
                    bu[rows_of(j), cols] * slab[g, rows_of(j), :]
                    * _silu(bga[rows_of(j), cols])).astype(bf16)
        return run

    ba = brg_ref[0:1, :]
    bx = brg_ref[1:2, :]
    cneg = _log2_decay(lam_ref[...])
    state = [hst[...]]

    def p4b(j):
        def run():
            xc = bxc[rows_of(j), :]
            r = _sigmoid(brp[rows_of(j), :] + ba)
            ig = _sigmoid(bip[rows_of(j), :] + bx)
            a = jnp.exp2(r * cneg)
            xs = _sqrt_unit_interval(1.0 - a * a) * (ig * xc)
            h = state[0]
            ys = []
            for t in range(RB // NB):
                h = a[t * NB:(t + 1) * NB] * h + xs[t * NB:(t + 1) * NB]
                ys.append(h)
            state[0] = h
            yr = jnp.concatenate(ys, axis=0)
            yb_s[rows_of(j), :] = (yr * bgb[rows_of(j), :]).astype(bf16)
        return run

    def gb_act(q):
        def run():
            cols = slice(q * GBLK, (q + 1) * GBLK)
            bgb[:, cols] = _silu(bgb[:, cols])
        return run

    def scan_done():
        hst[...] = state[0]
        hl_ref[...] = state[0]

    def p5z(j):
        def run():
            bv[rows_of(j), :] = _sigmoid(bv[rows_of(j), :])
            bpa[rows_of(j), :] = _sigmoid(bpa[rows_of(j), :])
        return run

    def p5a(j):
        def run():
            bu[rows_of(j), :] = bv[rows_of(j), :] * bu[rows_of(j), :]
        return run

    def p5b(q):
        def run():
            cols = slice(q * GBLK, (q + 1) * GBLK)
            for j in range(NRB):
                mg = bu[rows_of(j), cols] + bpa[rows_of(j), cols] * brp[rows_of(j), cols]
                if q == 0 and j == 0 and anchors:
                    acc = anchors[0]
                    for piece in anchors[1:]:
                        acc = acc + piece
                    bits = lax.bitcast_convert_type(acc, jnp.uint32)
                    zero = lax.bitcast_convert_type(
                        lax.shift_right_logical(
                            lax.shift_right_logical(bits, jnp.uint32(16)), jnp.uint32(16)),
                        f32)
                    mg = mg + jnp.tile(zero, (RB // 8, GBLK // GW))
                xcb_s[rows_of(j), cols] = mg.astype(bf16)
        return run

    fg = fg_ref[...]

    def p6(j):
        def run():
            o = bgb[rows_of(j), :].reshape(8, NB, D)
            y = load_x(0, j) + gate[None] * o
            if final:
                y = (y * _rms_scale(y) * fg).reshape(RB, D)
                for c in range(D // GW):
                    stag[c, rows_of(j), :] = y[:, c * GW:(c + 1) * GW]
            else:
                out_ref[j * 8:(j + 1) * 8] = y
        return run

    def p6_cols(n):
        def run():
            cols = slice(n * GBLK, (n + 1) * GBLK)
            for j in range(NRB):
                if first:
                    xv = jnp.concatenate(
                        [stag[slot, c, rows_of(j), :]
                         for c in range(n * (GBLK // GW), (n + 1) * (GBLK // GW))], axis=1)
                    xv = xv.reshape(8, NB, GBLK)
                else:
                    xv = x_refs[0][j * 8:(j + 1) * 8, :, cols]
                o = bgb[rows_of(j), cols].reshape(8, NB, GBLK)
                out_ref[j * 8:(j + 1) * 8, :, cols] = xv + gate[None, :, cols] * o
        return run

    def unstage(b):
        def run():
            for c in range(D // GW):
                out_ref[b, :, c * GW:(c + 1) * GW] = stag[c, pl.ds(b, TT, stride=NB), :]
        return run

    def units(f):
        return [f(j) for j in range(NRB)]

    for src, dst in zip(next_f32, next_bf16):
        dst[...] = src[...].astype(bf16)

    interleave(proj(xb_s, C_XB, TAIL), stage(x_refs[NB:], 1 - slot) if first else [])
    interleave(proj(bgb, C_GB), units(p4a) + [conv_tail])
    interleave(proj(bu, C_U), [gb_act(q) for q in range(NGB)])
    interleave([gates(q) for q in range(NGB)], [])
    interleave(proj(bv, C_V), units(p4b)[:NRB // 2])
    interleave(proj(bga, C_GA), units(p4b)[NRB // 2:] + [scan_done])
    interleave(proj(bpa, C_ZB), units(p2) + [relayout(g) for g in range(NG)])
    interleave([spatial(g) for g in range(NG)], [])
    interleave(proj(bv, C_ZA), units(p3))
    interleave(mm(bu, ya_s, wpa_ref), units(p5z) + prenorm(1, 1 - slot))
    for f in units(p5a):
        f()
    for chunk, merge in zip(mm(brp, yb_s, wpb_ref), [p5b(q) for q in range(NGB)]):
        chunk()
        merge()
    if final:
        interleave(mm(bgb, xcb_s, wout_ref), [])
        for f in units(p6):
            f()
        for b in range(NB):
            unstage(b)()
    else:
        for chunk, resid in zip(mm(bgb, xcb_s, wout_ref), [p6_cols(n) for n in range(NGB)]):
            chunk()
            resid()

    @pl.when(i == NT - 1)
    def _():
        _sample_layer(final, xs_ref, mods_ref, h0_ref, cbuf_ref, ws0_ref, bs0_ref, ng_ref,
                      win_ref, vng_ref, cw_ref, cb_ref, wga_ref, wgx_ref, brg_ref, lam_ref,
                      wpa_ref, wpb_ref, wout_ref, fg_ref, xso_ref, hs_ref, cs_ref, vs_ref)


def _sample_layer(final, x_ref, mod_ref, h0_ref, cbuf_ref, ws0_ref, bs0_ref, ng_ref, win_ref,
                  vng_ref, cw_ref, cb_ref, wga_ref, wgx_ref, brg_ref, lam_ref, wpa_ref,
                  wpb_ref, wout_ref, fg_ref, xo_ref, hs_ref, cs_ref, vs_ref):
    x = x_ref[...]
    shift = mod_ref[:, 0:D]
    scale = mod_ref[:, D:2 * D]
    gate = mod_ref[:, 2 * D:3 * D]
    h = ((x * _rms_scale(x) * ng_ref[...]) * (1.0 + scale) + shift).astype(bf16)

    def proj(c0):
        return jnp.dot(h, win_ref[:, c0:c0 + D], preferred_element_type=f32)

    v = proj(C_V)
    v = v * _rms_scale(v) * vng_ref[...]
    vs_ref[...] = v
    s = ws0_ref[...] * v + bs0_ref[...]
    ya = (proj(C_U) * s * _silu(proj(C_GA))).astype(bf16)

    xb = proj(C_XB)
    xc = (cb_ref[...] + cw_ref[0:1, :] * cbuf_ref[0] + cw_ref[1:2, :] * cbuf_ref[1]
          + cw_ref[2:3, :] * cbuf_ref[2] + cw_ref[3:4, :] * xb)
    cs_ref[0] = cbuf_ref[1]
    cs_ref[1] = cbuf_ref[2]
    cs_ref[2] = xb
    xcb = xc.astype(bf16)
    rp = jnp.concatenate(
        [jnp.dot(xcb[:, q * GBLK:(q + 1) * GBLK], wga_ref[q], preferred_element_type=f32)
         for q in range(NGB)], axis=1)
    ip = jnp.concatenate(
        [jnp.dot(xcb[:, q * GBLK:(q + 1) * GBLK], wgx_ref[q], preferred_element_type=f32)
         for q in range(NGB)], axis=1)
    r = _sigmoid(rp + brg_ref[0:1, :])
    ig = _sigmoid(ip + brg_ref[1:2, :])
    a = jnp.exp2(r * _log2_decay(lam_ref[...]))
    hn = a * h0_ref[...] + _sqrt_unit_interval(1.0 - a * a) * (ig * xc)
    hs_ref[...] = hn
    yb = (hn * _silu(proj(C_GB))).astype(bf16)

    pa = jnp.dot(ya, wpa_ref[...], preferred_element_type=f32)
    pb = jnp.dot(yb, wpb_ref[...], preferred_element_type=f32)
    mg = (_sigmoid(proj(C_ZA)) * pa + _sigmoid(proj(C_ZB)) * pb).astype(bf16)
    xn = x + gate * jnp.dot(mg, wout_ref[...], preferred_element_type=f32)
    if final:
        xn = xn * _rms_scale(xn) * fg_ref[...]
    xo_ref[...] = xn


def _layer_const_spec(shape, l):
    nd = len(shape)
    return pl.BlockSpec((None,) + tuple(shape[1:]), lambda i: (l,) + (0,) * (nd - 1),
                        pipeline_mode=pl.Buffered(1))


def _layer(l, x_in, x_s, mod, ng, win, vng, ws, bs, cw, cb, wga, wgx, brg, lam,
           wpa, wpb, wout, fg, h0, cbuf, ws0, bs0, next_f32):
    final = l == DEPTH - 1
    own = {id(win), id(wpa), id(wpb), id(wout)}
    consts = (ng, win, vng, ws, bs, cw, cb, wga, wgx, brg, lam, wpa, wpb, wout)
    const_specs = [
        pl.BlockSpec(a.shape, lambda i: (0, 0), pipeline_mode=pl.Buffered(1))
        if id(a) in own else _layer_const_spec(a.shape, l) for a in consts]
    sample_stacked = (h0, cbuf, ws0, bs0)
    rows = D // NT
    if final:
        next_f32, cast_in_specs, cast_out_specs, cast_shapes = (), [], [], []
    else:
        cast_in_specs = [pl.BlockSpec((None, rows, w.shape[2]), lambda i: (l + 1, i, 0))
                         for w in next_f32]
        cast_out_specs = [pl.BlockSpec((rows, w.shape[2]), lambda i: (i, 0))
                          for w in next_f32]
        cast_shapes = [jax.ShapeDtypeStruct(w.shape[1:], bf16) for w in next_f32]
    xs_spec = pl.BlockSpec((NS, D), lambda i: (0, 0), pipeline_mode=pl.Buffered(1))
    mods_spec = pl.BlockSpec((None, NS, 3 * D), lambda i: (l, 0, 0),
                             pipeline_mode=pl.Buffered(1))
    mod_spec = pl.BlockSpec((None, NB, 3 * D), lambda i: (l, NS // NB, 0),
                            pipeline_mode=pl.Buffered(1))
    fg_spec = pl.BlockSpec((1, D), lambda i: (0, 0), pipeline_mode=pl.Buffered(1))
    first = l == 0
    if first:
        x_specs = ([pl.BlockSpec((None, TT, D), lambda i, b=b: (b, 0, 0),
                                 pipeline_mode=pl.Buffered(1)) for b in range(NB)]
                   + [pl.BlockSpec((None, TT, D),
                                   lambda i, b=b: (b, jnp.minimum(i + 1, NT - 1), 0))
                      for b in range(NB)])
        extra_scratch = [pltpu.VMEM((2, D // GW, TM, GW), f32)]
    else:
        x_specs = [pl.BlockSpec((TT, NB, D), lambda i: (i, 0, 0)),
                   pl.BlockSpec((TT, NB, D), lambda i: (jnp.minimum(i + 1, NT - 1), 0, 0))]
        extra_scratch = []
    if final:
        y_spec = pl.BlockSpec((NB, TT, D), lambda i: (0, i, 0))
        y_shape = jax.ShapeDtypeStruct((NB, SEQ, D), f32)
        extra_scratch = [pltpu.VMEM((D // GW, TM, GW), f32)]
    else:
        y_spec = pl.BlockSpec((TT, NB, D), lambda i: (i, 0, 0))
        y_shape = jax.ShapeDtypeStruct((SEQ, NB, D), f32)
    return pl.pallas_call(
        functools.partial(_prompt_layer_kernel, final, first),
        grid=(NT,),
        in_specs=x_specs
        + [mod_spec] + const_specs + [fg_spec]
        + [xs_spec, mods_spec] + [_layer_const_spec(a.shape, l) for a in sample_stacked]
        + cast_in_specs,
        out_specs=[
            y_spec,
            pl.BlockSpec((NB, D), lambda i: (0, 0)),
            pl.BlockSpec((TAIL, D), lambda i: (0, 0)),
            pl.BlockSpec((NS, D), lambda i: (0, 0)),
            pl.BlockSpec((NS, D), lambda i: (0, 0)),
            pl.BlockSpec((CONV_W - 1, NS, D), lambda i: (0, 0, 0)),
            pl.BlockSpec((NS, D), lambda i: (0, 0)),
        ] + cast_out_specs,
        out_shape=[
            y_shape,
            jax.ShapeDtypeStruct((NB, D), f32),
            jax.ShapeDtypeStruct((TAIL, D), f32),
            jax.ShapeDtypeStruct((NS, D), f32),
            jax.ShapeDtypeStruct((NS, D), f32),
            jax.ShapeDtypeStruct((CONV_W - 1, NS, D), f32),
            jax.ShapeDtypeStruct((NS, D), f32),
        ] + cast_shapes,
        scratch_shapes=[
            pltpu.VMEM((2, TM, D), bf16),
            pltpu.VMEM((TM, D), f32),
            pltpu.VMEM((TM, D), f32),
            pltpu.VMEM((TM, D), f32),
            pltpu.VMEM((TM, D), f32),
            pltpu.VMEM((TM, D), f32),
            pltpu.VMEM((TM, D), f32),
            pltpu.VMEM((TM, D), f32),
            pltpu.VMEM((TM, D), f32),
            pltpu.VMEM((NG, TM, GW), f32),
            pltpu.VMEM((NG, CHUNK, NB * GW), bf16),
            pltpu.VMEM((TM + TAIL, D), f32),
            pltpu.VMEM((NB, D), f32),
            pltpu.VMEM((TM, D), bf16),
            pltpu.VMEM((TM, D), bf16),
            pltpu.VMEM((TM, D), bf16),
        ] + extra_scratch,
        compiler_params=pltpu.CompilerParams(
            dimension_semantics=("arbitrary",), vmem_limit_bytes=VMEM_LIMIT),
        name="layer_final" if final else ("layer_first" if first else "layer"),
    )(*([x_in] * len(x_specs)), mod, *consts, fg, x_s, mod, *sample_stacked, *next_f32)


def _block_diag_gate(w):
    per = GBLK // BW
    w = w.reshape(DEPTH, NGB, per, BW, BW)
    eye = jnp.eye(per, dtype=w.dtype)
    out = jnp.einsum('lqhij,hk->lqhikj', w, eye)
    return out.reshape(DEPTH, NGB, GBLK, GBLK)


def kernel(x_prompt, x_sample, c_prompt, c_sample, state_rglru_h, state_conv, w_ada, b_ada,
           norm_g, w_in, v_norm_g, w_s, b_s, conv_w, conv_b, w_rg_a, b_rg_a, w_rg_x,
           b_rg_x, lam, w_pa, w_pb, w_out, final_g):
    c_all = jnp.concatenate([c_sample, c_prompt, jnp.zeros((8, D), f32)], axis=0)
    mod = _modulation(c_all, w_ada, b_ada)

    big_f32 = (w_in, w_pa, w_pb, w_out)
    win, wpa, wpb, wout = (w[0].astype(bf16) for w in big_f32)
    wga = _block_diag_gate(w_rg_a).astype(bf16)
    wgx = _block_diag_gate(w_rg_x).astype(bf16)
    ws = w_s.astype(bf16)
    bs = jnp.broadcast_to(b_s[..., None], (DEPTH, NG, CHUNK, GW))
    ng = norm_g.reshape(DEPTH, 1, D)
    vng = v_norm_g.reshape(DEPTH, 1, D)
    cb = conv_b.reshape(DEPTH, 1, D)
    brg = jnp.stack([b_rg_a, b_rg_x], axis=1)
    lam3 = lam.reshape(DEPTH, 1, D)
    fg = final_g.reshape(1, D)

    ws0 = jnp.repeat(w_s[:, :, 0, 0], GW, axis=-1).reshape(DEPTH, 1, D)
    bs0 = jnp.repeat(b_s[:, :, 0], GW, axis=-1).reshape(DEPTH, 1, D)
    cbuf = state_conv.transpose(0, 2, 1, 3)

    x_p = x_prompt
    x_s = x_sample.reshape(NS, D)
    hp, cp, hs, cs, vs = [], [], [], [], []
    for l in range(DEPTH):
        x_p, h_l, c_l, x_s, hs_l, cs_l, vs_l, *next_bf16 = _layer(
            l, x_p, x_s, mod, ng, win, vng, ws, bs, conv_w, cb, wga, wgx, brg, lam3,
            wpa, wpb, wout, fg, state_rglru_h, cbuf, ws0, bs0, big_f32)
        if next_bf16:
            win, wpa, wpb, wout = next_bf16
        hp.append(h_l)
        cp.append(c_l.reshape(CONV_W - 1, NB, D).transpose(1, 0, 2))
        hs.append(hs_l)
        cs.append(cs_l.transpose(1, 0, 2))
        vs.append(vs_l)
    y_prompt = x_p
    h_prompt = jnp.stack(hp)
    conv_prompt = jnp.stack(cp)
    y_sample = x_s.reshape(NS, 1, D)
    h_sample = jnp.stack(hs)
    conv_sample = jnp.stack(cs)
    chunk_v_sample = jnp.stack(vs).reshape(DEPTH, NS, 1, D)
    return (y_prompt, y_sample, h_prompt, conv_prompt, h_sample, conv_sample,
            chunk_v_sample)
```

```python
import functools

import jax
import jax.numpy as jnp
from jax import lax
from jax.experimental import pallas as pl
from jax.experimental.pallas import tpu as pltpu

D = 1024
NB = 8
SEQ = 2048
DEPTH = 4
NS = 128
CHUNK = 128
GW = 128
NG = D // GW
HB = 16
BW = D // HB
CONV_W = 4
C_RG = 8.0
EPS = 1e-6
D_IN = 7 * D
C_U, C_V, C_GA, C_XB, C_GB, C_ZA, C_ZB = (i * D for i in range(7))

TT = 32
TM = TT * NB
NT = SEQ // TT
RB = 64
NRB = TM // RB
GBLK = 256
NGB = D // GBLK
TAIL = (CONV_W - 1) * NB

VMEM_LIMIT = 62 * 1024 * 1024

f32 = jnp.float32
bf16 = jnp.bfloat16


def _sigmoid(x):
    return 0.5 * jnp.tanh(0.5 * x) + 0.5


def _silu(x):
    return x * _sigmoid(x)


def _sqrt_unit_interval(x):
    return jnp.where(x == 0.0, 0.0, x * lax.rsqrt(x))


LOG2E = 1.4426950408889634


def _log2_decay(lam):
    y = -lam
    sp = jnp.maximum(y, 0.0) + jnp.log1p(jnp.exp(-jnp.abs(y)))
    return (-C_RG * LOG2E) * sp


def _rms_scale(x):
    return lax.rsqrt(jnp.mean(x * x, axis=-1, keepdims=True) + EPS)


def _as_operand(w):
    return pltpu.bitcast(w, bf16) if w.dtype == jnp.uint32 else w


def _mod_kernel(c_ref, w_ref, b_ref, o_ref):
    a = _silu(c_ref[...]).astype(bf16)
    w = w_ref[0].astype(bf16)
    o_ref[0] = jnp.dot(a, w, preferred_element_type=f32) + b_ref[0]


def _modulation(c_all, w_ada, b_ada):
    m = c_all.shape[0]
    return pl.pallas_call(
        _mod_kernel,
        grid=(DEPTH, 3),
        in_specs=[
            pl.BlockSpec((m, D), lambda l, j: (0, 0)),
            pl.BlockSpec((1, D, D), lambda l, j: (l, 0, j)),
            pl.BlockSpec((1, 1, D), lambda l, j: (l, 0, j)),
        ],
        out_specs=pl.BlockSpec((1, m, D), lambda l, j: (l, 0, j)),
        out_shape=jax.ShapeDtypeStruct((DEPTH, m, 3 * D), f32),
        compiler_params=pltpu.CompilerParams(
            dimension_semantics=("arbitrary", "arbitrary")),
        name="adaln_mod",
    )(c_all, w_ada, b_ada.reshape(DEPTH, 1, 3 * D))


def _prompt_layer_kernel(final, first, *refs):
    n_x = 2 * NB if first else 2
    x_refs, refs = refs[:n_x], refs[n_x:]
    (mod_ref, ng_ref, win_ref, vng_ref, ws_ref, bs_ref, cw_ref, cb_ref, wga_ref, wgx_ref,
     brg_ref, lam_ref, wpa_ref, wpb_ref, wout_ref, fg_ref,
     xs_ref, mods_ref, h0_ref, cbuf_ref, ws0_ref, bs0_ref) = refs[:22]
    refs = refs[22:]
    next_f32, refs = ((), refs) if final else (refs[:4], refs[4:])
    out_ref, hl_ref, cn_ref, xso_ref, hs_ref, cs_ref, vs_ref = refs[:7]
    refs = refs[7:]
    next_bf16, refs = ((), refs) if final else (refs[:4], refs[4:])
    (h_s, bv, bu, bga, bgb, bxc, brp, bip, bpa, slab, vprime, xb_s,
     hst, ya_s, yb_s, xcb_s) = refs[:16]
    stag = refs[16] if (first or final) else None
    i = pl.program_id(0)
    part = i % (CHUNK // TT)
    hrow = pl.multiple_of(part * TT, TT)
    slot = i % 2

    shift = mod_ref[:, 0:D]
    scale1 = 1.0 + mod_ref[:, D:2 * D]
    gate = mod_ref[:, 2 * D:3 * D]
    ng = ng_ref[...]

    def rows_of(j):
        return slice(j * RB, (j + 1) * RB)

    def stage(batch_refs, dst_slot):
        def unit(b):
            def run():
                for c in range(D // GW):
                    stag[dst_slot, c, pl.ds(b, TT, stride=NB), :] = (
                        batch_refs[b][:, c * GW:(c + 1) * GW])
            return run
        return [unit(b) for b in range(NB)]

    def load_x(which, j):
        if not first:
            return x_refs[which][j * 8:(j + 1) * 8]
        s = slot if which == 0 else 1 - slot
        xv = jnp.concatenate([stag[s, c, rows_of(j), :] for c in range(D // GW)], axis=1)
        return xv.reshape(8, NB, D)

    def prenorm(which, dst_slot):
        def unit(j):
            def run():
                xv = load_x(which, j)
                hv = (xv * _rms_scale(xv) * ng) * scale1[None] + shift[None]
                h_s[dst_slot, rows_of(j), :] = hv.reshape(RB, D).astype(bf16)
            return run
        return [unit(j) for j in range(NRB)]

    @pl.when(i == 0)
    def _():
        hst[...] = jnp.zeros_like(hst)
        xb_s[0:TAIL, :] = jnp.zeros((TAIL, D), f32)
        if first:
            for f in stage(x_refs[:NB], 0):
                f()
        for f in prenorm(0, 0):
            f()

    @pl.when(part == 0)
    def _():
        vprime[:, TT:CHUNK, :] = jnp.zeros((NG, CHUNK - TT, NB * GW), bf16)

    def mm(dst, lhs, w_ref, c0=0, r0=0):
        def chunk(n):
            def run():
                a = h_s[slot] if lhs is None else lhs[...]
                dst[r0:r0 + TM, n * GBLK:(n + 1) * GBLK] = jnp.dot(
                    a, _as_operand(w_ref[:, c0 + n * GBLK:c0 + (n + 1) * GBLK]),
                    preferred_element_type=f32)
            return run
        return [chunk(n) for n in range(NGB)]

    def proj(dst, c0, r0=0):
        return mm(dst, None, win_ref, c0, r0)

    def interleave(mxu, vpu):
        n, m = len(mxu), len(vpu)
        done = 0
        for k, f in enumerate(mxu):
            upto = (m * (k + 1)) // n
            for g in vpu[done:upto]:
                g()
            done = upto
            f()

    cb = cb_ref[...]
    cw = [cw_ref[k:k + 1, :] for k in range(CONV_W)]

    def p4a(j):
        def run():
            xc = cb
            for k in range(CONV_W):
                xc = xc + cw[k] * xb_s[j * RB + k * NB:(j + 1) * RB + k * NB, :]
            bxc[rows_of(j), :] = xc
            xcb_s[rows_of(j), :] = xc.astype(bf16)
        return run

    def conv_tail():
        tail = xb_s[TM:TM + TAIL, :]
        cn_ref[...] = tail
        xb_s[0:TAIL, :] = tail

    vng = vng_ref[...]

    def p2(j):
        def run():
            vv = bv[rows_of(j), :]
            vn = vv * _rms_scale(vv) * vng
            for g in range(NG):
                slab[g, rows_of(j), :] = vn[:, g * GW:(g + 1) * GW]
        return run

    def relayout(g):
        def run():
            for b in range(NB):
                vprime[g, pl.ds(hrow, TT), b * GW:(b + 1) * GW] = (
                    slab[g, pl.ds(b, TT, stride=NB), :].astype(bf16))
        return run

    def gates(q):
        def run():
            cols = slice(q * GBLK, (q + 1) * GBLK)
            brp[:, cols] = jnp.dot(xcb_s[:, cols], wga_ref[q], preferred_element_type=f32)
            bip[:, cols] = jnp.dot(xcb_s[:, cols], wgx_ref[q], preferred_element_type=f32)
        return run

    t_idx = hrow + lax.broadcasted_iota(jnp.int32, (TT, CHUNK), 0)
    s_idx = lax.broadcasted_iota(jnp.int32, (TT, CHUNK), 1)
    causal = s_idx <= t_idx

    def spatial(g):
        def run():
            wt = jnp.where(causal, ws_ref[g, pl.ds(hrow, TT), :], jnp.zeros((), bf16))
            sp = jnp.dot(wt, vprime[g], preferred_element_type=f32)
            bias = bs_ref[g, pl.ds(hrow, TT), :]
            for b in range(NB):
                slab[g, pl.ds(b, TT, stride=NB), :] = sp[:, b * GW:(b + 1) * GW] + bias
        return run

    def p3(j):
        def run():
            for g in range(NG):
                cols = slice(g * GW, (g + 1) * GW)
                ya_s[rows_of(j), cols] = (
                    bu[rows_of(j), cols] * slab[g, rows_of(j), :]
                    * _silu(bga[rows_of(j), cols])).astype(bf16)
        return run

    ba = brg_ref[0:1, :]
    bx = brg_ref[1:2, :]
    cneg = _log2_decay(lam_ref[...])
    state = [hst[...]]

    def p4b(j):
        def run():
            xc = bxc[rows_of(j), :]
            r = _sigmoid(brp[rows_of(j), :] + ba)
            ig = _sigmoid(bip[rows_of(j), :] + bx)
            a = jnp.exp2(r * cneg)
            xs = _sqrt_unit_interval(1.0 - a * a) * (ig * xc)
            h = state[0]
            ys = []
            for t in range(RB // NB):
                h = a[t * NB:(t + 1) * NB] * h + xs[t * NB:(t + 1) * NB]
                ys.append(h)
            state[0] = h
            yr = jnp.concatenate(ys, axis=0)
            yb_s[rows_of(j), :] = (yr * bgb[rows_of(j), :]).astype(bf16)
        return run

    def gb_act(q):
        def run():
            cols = slice(q * GBLK, (q + 1) * GBLK)
            bgb[:, cols] = _silu(bgb[:, cols])
        return run

    def scan_done():
        hst[...] = state[0]
        hl_ref[...] = state[0]

    def p5z(j):
        def run():
            bv[rows_of(j), :] = _sigmoid(bv[rows_of(j), :])
            bpa[rows_of(j), :] = _sigmoid(bpa[rows_of(j), :])
        return run

    def p5a(j):
        def run():
            bu[rows_of(j), :] = bv[rows_of(j), :] * bu[rows_of(j), :]
        return run

    def p5b(q):
        def run():
            cols = slice(q * GBLK, (q + 1) * GBLK)
            for j in range(NRB):
                xcb_s[rows_of(j), cols] = (
                    bu[rows_of(j), cols] + bpa[rows_of(j), cols] * brp[rows_of(j), cols]
                ).astype(bf16)
        return run

    fg = fg_ref[...]

    def p6(j):
        def run():
            o = bgb[rows_of(j), :].reshape(8, NB, D)
            y = load_x(0, j) + gate[None] * o
            if final:
                y = (y * _rms_scale(y) * fg).reshape(RB, D)
                for c in range(D // GW):
                    stag[c, rows_of(j), :] = y[:, c * GW:(c + 1) * GW]
            else:
                out_ref[j * 8:(j + 1) * 8] = y
        return run

    def unstage(b):
        def run():
            for c in range(D // GW):
                out_ref[b, :, c * GW:(c + 1) * GW] = stag[c, pl.ds(b, TT, stride=NB), :]
        return run

    def units(f):
        return [f(j) for j in range(NRB)]

    for src, dst in zip(next_f32, next_bf16):
        dst[...] = pltpu.bitcast(src[...].astype(bf16), jnp.uint32)

    interleave(proj(xb_s, C_XB, TAIL), stage(x_refs[NB:], 1 - slot) if first else [])
    interleave(proj(bgb, C_GB), units(p4a) + [conv_tail])
    interleave(proj(bu, C_U), [gb_act(q) for q in range(NGB)])
    interleave([gates(q) for q in range(NGB)], [])
    interleave(proj(bv, C_V), units(p4b)[:NRB // 2])
    interleave(proj(bga, C_GA), units(p4b)[NRB // 2:] + [scan_done])
    interleave(proj(bpa, C_ZB), units(p2) + [relayout(g) for g in range(NG)])
    interleave([spatial(g) for g in range(NG)], [])
    interleave(proj(bv, C_ZA), units(p3))
    interleave(mm(bu, ya_s, wpa_ref), units(p5z) + prenorm(1, 1 - slot))
    interleave(mm(brp, yb_s, wpb_ref), units(p5a))
    interleave([p5b(q) for q in range(NGB)], [])
    interleave(mm(bgb, xcb_s, wout_ref), [])
    for f in units(p6):
        f()
    if final:
        for b in range(NB):
            unstage(b)()

    @pl.when(i == NT - 1)
    def _():
        _sample_layer(final, xs_ref, mods_ref, h0_ref, cbuf_ref, ws0_ref, bs0_ref, ng_ref,
                      win_ref, vng_ref, cw_ref, cb_ref, wga_ref, wgx_ref, brg_ref, lam_ref,
                      wpa_ref, wpb_ref, wout_ref, fg_ref, xso_ref, hs_ref, cs_ref, vs_ref)


def _sample_layer(final, x_ref, mod_ref, h0_ref, cbuf_ref, ws0_ref, bs0_ref, ng_ref, win_ref,
                  vng_ref, cw_ref, cb_ref, wga_ref, wgx_ref, brg_ref, lam_ref, wpa_ref,
                  wpb_ref, wout_ref, fg_ref, xo_ref, hs_ref, cs_ref, vs_ref):
    x = x_ref[...]
    shift = mod_ref[:, 0:D]
    scale = mod_ref[:, D:2 * D]
    gate = mod_ref[:, 2 * D:3 * D]
    h = ((x * _rms_scale(x) * ng_ref[...]) * (1.0 + scale) + shift).astype(bf16)

    def proj(c0):
        return jnp.dot(h, _as_operand(win_ref[:, c0:c0 + D]), preferred_element_type=f32)

    v = proj(C_V)
    v = v * _rms_scale(v) * vng_ref[...]
    vs_ref[...] = v
    s = ws0_ref[...] * v + bs0_ref[...]
    ya = (proj(C_U) * s * _silu(proj(C_GA))).astype(bf16)

    xb = proj(C_XB)
    xc = (cb_ref[...] + cw_ref[0:1, :] * cbuf_ref[0] + cw_ref[1:2, :] * cbuf_ref[1]
          + cw_ref[2:3, :] * cbuf_ref[2] + cw_ref[3:4, :] * xb)
    cs_ref[0] = cbuf_ref[1]
    cs_ref[1] = cbuf_ref[2]
    cs_ref[2] = xb
    xcb = xc.astype(bf16)
    rp = jnp.concatenate(
        [jnp.dot(xcb[:, q * GBLK:(q + 1) * GBLK], wga_ref[q], preferred_element_type=f32)
         for q in range(NGB)], axis=1)
    ip = jnp.concatenate(
        [jnp.dot(xcb[:, q * GBLK:(q + 1) * GBLK], wgx_ref[q], preferred_element_type=f32)
         for q in range(NGB)], axis=1)
    r = _sigmoid(rp + brg_ref[0:1, :])
    ig = _sigmoid(ip + brg_ref[1:2, :])
    a = jnp.exp2(r * _log2_decay(lam_ref[...]))
    hn = a * h0_ref[...] + _sqrt_unit_interval(1.0 - a * a) * (ig * xc)
    hs_ref[...] = hn
    yb = (hn * _silu(proj(C_GB))).astype(bf16)

    pa = jnp.dot(ya, _as_operand(wpa_ref[...]), preferred_element_type=f32)
    pb = jnp.dot(yb, _as_operand(wpb_ref[...]), preferred_element_type=f32)
    mg = (_sigmoid(proj(C_ZA)) * pa + _sigmoid(proj(C_ZB)) * pb).astype(bf16)
    xn = x + gate * jnp.dot(mg, _as_operand(wout_ref[...]), preferred_element_type=f32)
    if final:
        xn = xn * _rms_scale(xn) * fg_ref[...]
    xo_ref[...] = xn


def _layer_const_spec(shape, l):
    nd = len(shape)
    return pl.BlockSpec((None,) + tuple(shape[1:]), lambda i: (l,) + (0,) * (nd - 1),
                        pipeline_mode=pl.Buffered(1))


def _layer(l, x_in, x_s, mod, ng, win, vng, ws, bs, cw, cb, wga, wgx, brg, lam,
           wpa, wpb, wout, fg, h0, cbuf, ws0, bs0, next_f32):
    final = l == DEPTH - 1
    own = {id(win), id(wpa), id(wpb), id(wout)}
    consts = (ng, win, vng, ws, bs, cw, cb, wga, wgx, brg, lam, wpa, wpb, wout)
    const_specs = [
        pl.BlockSpec(a.shape, lambda i: (0, 0), pipeline_mode=pl.Buffered(1))
        if id(a) in own else _layer_const_spec(a.shape, l) for a in consts]
    sample_stacked = (h0, cbuf, ws0, bs0)
    rows = D // NT
    if final:
        next_f32, cast_in_specs, cast_out_specs, cast_shapes = (), [], [], []
    else:
        cast_in_specs = [pl.BlockSpec((None, rows, w.shape[2]), lambda i: (l + 1, i, 0))
                         for w in next_f32]
        cast_out_specs = [pl.BlockSpec((rows // 2, w.shape[2]), lambda i: (i, 0))
                          for w in next_f32]
        cast_shapes = [jax.ShapeDtypeStruct((w.shape[1] // 2, w.shape[2]), jnp.uint32)
                       for w in next_f32]
    xs_spec = pl.BlockSpec((NS, D), lambda i: (0, 0), pipeline_mode=pl.Buffered(1))
    mods_spec = pl.BlockSpec((None, NS, 3 * D), lambda i: (l, 0, 0),
                             pipeline_mode=pl.Buffered(1))
    mod_spec = pl.BlockSpec((None, NB, 3 * D), lambda i: (l, NS // NB, 0),
                            pipeline_mode=pl.Buffered(1))
    fg_spec = pl.BlockSpec((1, D), lambda i: (0, 0), pipeline_mode=pl.Buffered(1))
    first = l == 0
    if first:
        x_specs = ([pl.BlockSpec((None, TT, D), lambda i, b=b: (b, 0, 0),
                                 pipeline_mode=pl.Buffered(1)) for b in range(NB)]
                   + [pl.BlockSpec((None, TT, D),
                                   lambda i, b=b: (b, jnp.minimum(i + 1, NT - 1), 0))
                      for b in range(NB)])
        extra_scratch = [pltpu.VMEM((2, D // GW, TM, GW), f32)]
    else:
        x_specs = [pl.BlockSpec((TT, NB, D), lambda i: (i, 0, 0)),
                   pl.BlockSpec((TT, NB, D), lambda i: (jnp.minimum(i + 1, NT - 1), 0, 0))]
        extra_scratch = []
    if final:
        y_spec = pl.BlockSpec((NB, TT, D), lambda i: (0, i, 0))
        y_shape = jax.ShapeDtypeStruct((NB, SEQ, D), f32)
        extra_scratch = [pltpu.VMEM((D // GW, TM, GW), f32)]
    else:
        y_spec = pl.BlockSpec((TT, NB, D), lambda i: (i, 0, 0))
        y_shape = jax.ShapeDtypeStruct((SEQ, NB, D), f32)
    return pl.pallas_call(
        functools.partial(_prompt_layer_kernel, final, first),
        grid=(NT,),
        in_specs=x_specs
        + [mod_spec] + const_specs + [fg_spec]
        + [xs_spec, mods_spec] + [_layer_const_spec(a.shape, l) for a in sample_stacked]
        + cast_in_specs,
        out_specs=[
            y_spec,
            pl.BlockSpec((NB, D), lambda i: (0, 0)),
            pl.BlockSpec((TAIL, D), lambda i: (0, 0)),
            pl.BlockSpec((NS, D), lambda i: (0, 0)),
            pl.BlockSpec((NS, D), lambda i: (0, 0)),
            pl.BlockSpec((CONV_W - 1, NS, D), lambda i: (0, 0, 0)),
            pl.BlockSpec((NS, D), lambda i: (0, 0)),
        ] + cast_out_specs,
        out_shape=[
            y_shape,
            jax.ShapeDtypeStruct((NB, D), f32),
            jax.ShapeDtypeStruct((TAIL, D), f32),
            jax.ShapeDtypeStruct((NS, D), f32),
            jax.ShapeDtypeStruct((NS, D), f32),
            jax.ShapeDtypeStruct((CONV_W - 1, NS, D), f32),
            jax.ShapeDtypeStruct((NS, D), f32),
        ] + cast_shapes,
        scratch_shapes=[
            pltpu.VMEM((2, TM, D), bf16),
            pltpu.VMEM((TM, D), f32),
            pltpu.VMEM((TM, D), f32),
            pltpu.VMEM((TM, D), f32),
            pltpu.VMEM((TM, D), f32),
            pltpu.VMEM((TM, D), f32),
            pltpu.VMEM((TM, D), f32),
            pltpu.VMEM((TM, D), f32),
            pltpu.VMEM((TM, D), f32),
            pltpu.VMEM((NG, TM, GW), f32),
            pltpu.VMEM((NG, CHUNK, NB * GW), bf16),
            pltpu.VMEM((TM + TAIL, D), f32),
            pltpu.VMEM((NB, D), f32),
            pltpu.VMEM((TM, D), bf16),
            pltpu.VMEM((TM, D), bf16),
            pltpu.VMEM((TM, D), bf16),
        ] + extra_scratch,
        compiler_params=pltpu.CompilerParams(
            dimension_semantics=("arbitrary",), vmem_limit_bytes=VMEM_LIMIT),
        name="layer_final" if final else ("layer_first" if first else "layer"),
    )(*([x_in] * len(x_specs)), mod, *consts, fg, x_s, mod, *sample_stacked, *next_f32)


def _block_diag_gate(w):
    per = GBLK // BW
    w = w.reshape(DEPTH, NGB, per, BW, BW)
    eye = jnp.eye(per, dtype=w.dtype)
    out = jnp.einsum('lqhij,hk->lqhikj', w, eye)
    return out.reshape(DEPTH, NGB, GBLK, GBLK)


def kernel(x_prompt, x_sample, c_prompt, c_sample, state_rglru_h, state_conv, w_ada, b_ada,
           norm_g, w_in, v_norm_g, w_s, b_s, conv_w, conv_b, w_rg_a, b_rg_a, w_rg_x,
           b_rg_x, lam, w_pa, w_pb, w_out, final_g):
    c_all = jnp.concatenate([c_sample, c_prompt, jnp.zeros((8, D), f32)], axis=0)
    mod = _modulation(c_all, w_ada, b_ada)

    big_f32 = (w_in, w_pa, w_pb, w_out)
    win, wpa, wpb, wout = (w[0].astype(bf16) for w in big_f32)
    wga = _block_diag_gate(w_rg_a).astype(bf16)
    wgx = _block_diag_gate(w_rg_x).astype(bf16)
    ws = w_s.astype(bf16)
    bs = jnp.broadcast_to(b_s[..., None], (DEPTH, NG, CHUNK, GW))
    ng = norm_g.reshape(DEPTH, 1, D)
    vng = v_norm_g.reshape(DEPTH, 1, D)
    cb = conv_b.reshape(DEPTH, 1, D)
    brg = jnp.stack([b_rg_a, b_rg_x], axis=1)
    lam3 = lam.reshape(DEPTH, 1, D)
    fg = final_g.reshape(1, D)

    ws0 = jnp.repeat(w_s[:, :, 0, 0], GW, axis=-1).reshape(DEPTH, 1, D)
    bs0 = jnp.repeat(b_s[:, :, 0], GW, axis=-1).reshape(DEPTH, 1, D)
    cbuf = state_conv.transpose(0, 2, 1, 3)

    x_p = x_prompt
    x_s = x_sample.reshape(NS, D)
    hp, cp, hs, cs, vs = [], [], [], [], []
    for l in range(DEPTH):
        x_p, h_l, c_l, x_s, hs_l, cs_l, vs_l, *next_bf16 = _layer(
            l, x_p, x_s, mod, ng, win, vng, ws, bs, conv_w, cb, wga, wgx, brg, lam3,
            wpa, wpb, wout, fg, state_rglru_h, cbuf, ws0, bs0, big_f32)
        if next_bf16:
            win, wpa, wpb, wout = next_bf16
        hp.append(h_l)
        cp.append(c_l.reshape(CONV_W - 1, NB, D).transpose(1, 0, 2))
        hs.append(hs_l)
        cs.append(cs_l.transpose(1, 0, 2))
        vs.append(vs_l)
    y_prompt = x_p
    h_prompt = jnp.stack(hp)
    conv_prompt = jnp.stack(cp)
    y_sample = x_s.reshape(NS, 1, D)
    h_sample = jnp.stack(hs)
    conv_sample = jnp.stack(cs)
    chunk_v_sample = jnp.stack(vs).reshape(DEPTH, NS, 1, D)
    return (y_prompt, y_sample, h_prompt, conv_prompt, h_sample, conv_sample,
            chunk_v_sample)
```

```python
import functools

import jax
import jax.numpy as jnp
from jax import lax
from jax.experimental import pallas as pl
from jax.experimental.pallas import tpu as pltpu

D = 1024
NB = 8
SEQ = 2048
DEPTH = 4
NS = 128
CHUNK = 128
GW = 128
NG = D // GW
HB = 16
BW = D // HB
CONV_W = 4
C_RG = 8.0
EPS = 1e-6
D_IN = 7 * D
C_U, C_V, C_GA, C_XB, C_GB, C_ZA, C_ZB = (i * D for i in range(7))

TT = 32
TM = TT * NB
NT = SEQ // TT
RB = 64
NRB = TM // RB
GBLK = 256
NGB = D // GBLK
TAIL = (CONV_W - 1) * NB

VMEM_LIMIT = 62 * 1024 * 1024

f32 = jnp.float32
bf16 = jnp.bfloat16


def _sigmoid(x):
    return 0.5 * jnp.tanh(0.5 * x) + 0.5


def _silu(x):
    return x * _sigmoid(x)


def _sqrt_unit_interval(x):
    return jnp.where(x == 0.0, 0.0, x * lax.rsqrt(x))


LOG2E = 1.4426950408889634


def _log2_decay(lam):
    y = -lam
    sp = jnp.maximum(y, 0.0) + jnp.log1p(jnp.exp(-jnp.abs(y)))
    return (-C_RG * LOG2E) * sp


def _rms_scale(x):
    return lax.rsqrt(jnp.mean(x * x, axis=-1, keepdims=True) + EPS)


def _as_operand(w):
    return pltpu.bitcast(w, bf16) if w.dtype == jnp.uint32 else w


def _mod_kernel(c_ref, w_ref, b_ref, o_ref):
    a = _silu(c_ref[...]).astype(bf16)
    w = w_ref[0].astype(bf16)
    o_ref[0] = jnp.dot(a, w, preferred_element_type=f32) + b_ref[0]


def _modulation(c_all, w_ada, b_ada):
    m = c_all.shape[0]
    return pl.pallas_call(
        _mod_kernel,
        grid=(DEPTH, 3),
        in_specs=[
            pl.BlockSpec((m, D), lambda l, j: (0, 0)),
            pl.BlockSpec((1, D, D), lambda l, j: (l, 0, j)),
            pl.BlockSpec((1, 1, D), lambda l, j: (l, 0, j)),
        ],
        out_specs=pl.BlockSpec((1, m, D), lambda l, j: (l, 0, j)),
        out_shape=jax.ShapeDtypeStruct((DEPTH, m, 3 * D), f32),
        compiler_params=pltpu.CompilerParams(
            dimension_semantics=("arbitrary", "arbitrary")),
        name="adaln_mod",
    )(c_all, w_ada, b_ada.reshape(DEPTH, 1, 3 * D))


def _prompt_layer_kernel(final, first, *refs):
    n_x = 2 * NB if first else 2
    x_refs, refs = refs[:n_x], refs[n_x:]
    (mod_ref, ng_ref, win_ref, vng_ref, ws_ref, bs_ref, cw_ref, cb_ref, wga_ref, wgx_ref,
     brg_ref, lam_ref, wpa_ref, wpb_ref, wout_ref, fg_ref,
     xs_ref, mods_ref, h0_ref, cbuf_ref, ws0_ref, bs0_ref) = refs[:22]
    refs = refs[22:]
    next_f32, refs = ((), refs) if final else (refs[:4], refs[4:])
    out_ref, hl_ref, cn_ref, xso_ref, hs_ref, cs_ref, vs_ref = refs[:7]
    refs = refs[7:]
    next_bf16, refs = ((), refs) if final else (refs[:4], refs[4:])
    (h_s, bv, bu, bga, bgb, bxc, brp, bip, bpa, slab, vprime, xb_s,
     hst, ya_s, yb_s, xcb_s) = refs[:16]
    stag = refs[16] if (first or final) else None
    i = pl.program_id(0)
    part = i % (CHUNK // TT)
    hrow = pl.multiple_of(part * TT, TT)
    slot = i % 2

    shift = mod_ref[:, 0:D]
    scale1 = 1.0 + mod_ref[:, D:2 * D]
    gate = mod_ref[:, 2 * D:3 * D]
    ng = ng_ref[...]

    def rows_of(j):
        return slice(j * RB, (j + 1) * RB)

    def stage(batch_refs, dst_slot):
        def unit(b):
            def run():
                for c in range(D // GW):
                    stag[dst_slot, c, pl.ds(b, TT, stride=NB), :] = (
                        batch_refs[b][:, c * GW:(c + 1) * GW])
            return run
        return [unit(b) for b in range(NB)]

    def load_x(which, j):
        if not first:
            return x_refs[which][j * 8:(j + 1) * 8]
        s = slot if which == 0 else 1 - slot
        xv = jnp.concatenate([stag[s, c, rows_of(j), :] for c in range(D // GW)], axis=1)
        return xv.reshape(8, NB, D)

    def prenorm(which, dst_slot):
        def unit(j):
            def run():
                xv = load_x(which, j)
                hv = (xv * _rms_scale(xv) * ng) * scale1[None] + shift[None]
                h_s[dst_slot, rows_of(j), :] = hv.reshape(RB, D).astype(bf16)
            return run
        return [unit(j) for j in range(NRB)]

    @pl.when(i == 0)
    def _():
        hst[...] = jnp.zeros_like(hst)
        xb_s[0:TAIL, :] = jnp.zeros((TAIL, D), f32)
        if first:
            for f in stage(x_refs[:NB], 0):
                f()
        for f in prenorm(0, 0):
            f()

    @pl.when(part == 0)
    def _():
        vprime[:, TT:CHUNK, :] = jnp.zeros((NG, CHUNK - TT, NB * GW), bf16)

    def mm(dst, lhs, w_ref, c0=0, r0=0):
        def chunk(n):
            def run():
                a = h_s[slot] if lhs is None else lhs[...]
                dst[r0:r0 + TM, n * GBLK:(n + 1) * GBLK] = jnp.dot(
                    a, _as_operand(w_ref[:, c0 + n * GBLK:c0 + (n + 1) * GBLK]),
                    preferred_element_type=f32)
            return run
        return [chunk(n) for n in range(NGB)]

    def proj(dst, c0, r0=0):
        return mm(dst, None, win_ref, c0, r0)

    def interleave(mxu, vpu):
        n, m = len(mxu), len(vpu)
        done = 0
        for k, f in enumerate(mxu):
            upto = (m * (k + 1)) // n
            for g in vpu[done:upto]:
                g()
            done = upto
            f()

    cb = cb_ref[...]
    cw = [cw_ref[k:k + 1, :] for k in range(CONV_W)]

    def p4a(j):
        def run():
            xc = cb
            for k in range(CONV_W):
                xc = xc + cw[k] * xb_s[j * RB + k * NB:(j + 1) * RB + k * NB, :]
            bxc[rows_of(j), :] = xc
            xcb_s[rows_of(j), :] = xc.astype(bf16)
        return run

    def conv_tail():
        tail = xb_s[TM:TM + TAIL, :]
        cn_ref[...] = tail
        xb_s[0:TAIL, :] = tail

    vng = vng_ref[...]

    def p2(j):
        def run():
            vv = bv[rows_of(j), :]
            vn = vv * _rms_scale(vv) * vng
            for g in range(NG):
                slab[g, rows_of(j), :] = vn[:, g * GW:(g + 1) * GW]
        return run

    def relayout(g):
        def run():
            for b in range(NB):
                vprime[g, pl.ds(hrow, TT), b * GW:(b + 1) * GW] = (
                    slab[g, pl.ds(b, TT, stride=NB), :].astype(bf16))
        return run

    def gates(q):
        def run():
            cols = slice(q * GBLK, (q + 1) * GBLK)
            brp[:, cols] = jnp.dot(xcb_s[:, cols], wga_ref[q], preferred_element_type=f32)
            bip[:, cols] = jnp.dot(xcb_s[:, cols], wgx_ref[q], preferred_element_type=f32)
        return run

    t_idx = hrow + lax.broadcasted_iota(jnp.int32, (TT, CHUNK), 0)
    s_idx = lax.broadcasted_iota(jnp.int32, (TT, CHUNK), 1)
    causal = s_idx <= t_idx

    def spatial(g):
        def run():
            wt = jnp.where(causal, ws_ref[g, pl.ds(hrow, TT), :], jnp.zeros((), bf16))
            sp = jnp.dot(wt, vprime[g], preferred_element_type=f32)
            bias = bs_ref[g, pl.ds(hrow, TT), :]
            for b in range(NB):
                slab[g, pl.ds(b, TT, stride=NB), :] = sp[:, b * GW:(b + 1) * GW] + bias
        return run

    def p3(j):
        def run():
            for g in range(NG):
                cols = slice(g * GW, (g + 1) * GW)
                ya_s[rows_of(j), cols] = (
                    bu[rows_of(j), cols] * slab[g, rows_of(j), :]
                    * _silu(bga[rows_of(j), cols])).astype(bf16)
        return run

    ba = brg_ref[0:1, :]
    bx = brg_ref[1:2, :]
    cneg = _log2_decay(lam_ref[...])
    state = [hst[...]]

    def p4b(j):
        def run():
            xc = bxc[rows_of(j), :]
            r = _sigmoid(brp[rows_of(j), :] + ba)
            ig = _sigmoid(bip[rows_of(j), :] + bx)
            a = jnp.exp2(r * cneg)
            xs = _sqrt_unit_interval(1.0 - a * a) * (ig * xc)
            h = state[0]
            ys = []
            for t in range(RB // NB):
                h = a[t * NB:(t + 1) * NB] * h + xs[t * NB:(t + 1) * NB]
                ys.append(h)
            state[0] = h
            yr = jnp.concatenate(ys, axis=0)
            yb_s[rows_of(j), :] = (yr * bgb[rows_of(j), :]).astype(bf16)
        return run

    def gb_act(q):
        def run():
            cols = slice(q * GBLK, (q + 1) * GBLK)
            bgb[:, cols] = _silu(bgb[:, cols])
        return run

    def scan_done():
        hst[...] = state[0]
        hl_ref[...] = state[0]

    def p5z(j):
        def run():
            bv[rows_of(j), :] = _sigmoid(bv[rows_of(j), :])
            bpa[rows_of(j), :] = _sigmoid(bpa[rows_of(j), :])
        return run

    def p5a(j):
        def run():
            bu[rows_of(j), :] = bv[rows_of(j), :] * bu[rows_of(j), :]
        return run

    def p5b(q):
        def run():
            cols = slice(q * GBLK, (q + 1) * GBLK)
            for j in range(NRB):
                xcb_s[rows_of(j), cols] = (
                    bu[rows_of(j), cols] + bpa[rows_of(j), cols] * brp[rows_of(j), cols]
                ).astype(bf16)
        return run

    fg = fg_ref[...]

    def p6(j):
        def run():
            o = bgb[rows_of(j), :].reshape(8, NB, D)
            y = load_x(0, j) + gate[None] * o
            if final:
                y = (y * _rms_scale(y) * fg).reshape(RB, D)
                for c in range(D // GW):
                    stag[c, rows_of(j), :] = y[:, c * GW:(c + 1) * GW]
            else:
                out_ref[j * 8:(j + 1) * 8] = y
        return run

    def unstage(b):
        def run():
            for c in range(D // GW):
                out_ref[b, :, c * GW:(c + 1) * GW] = stag[c, pl.ds(b, TT, stride=NB), :]
        return run

    def units(f):
        return [f(j) for j in range(NRB)]

    for src, dst in zip(next_f32, next_bf16):
        dst[...] = pltpu.bitcast(src[...].astype(bf16), jnp.uint32)

    interleave(proj(xb_s, C_XB, TAIL), stage(x_refs[NB:], 1 - slot) if first else [])
    interleave(proj(bgb, C_GB), units(p4a) + [conv_tail])
    interleave(proj(bu, C_U), [gb_act(q) for q in range(NGB)])
    interleave([gates(q) for q in range(NGB)], [])
    interleave(proj(bv, C_V), units(p4b)[:NRB // 2])
    interleave(proj(bga, C_GA), units(p4b)[NRB // 2:] + [scan_done])
    interleave(proj(bpa, C_ZB), units(p2) + [relayout(g) for g in range(NG)])
    interleave([spatial(g) for g in range(NG)], [])
    interleave(proj(bv, C_ZA), units(p3))
    interleave(mm(bu, ya_s, wpa_ref), units(p5z) + prenorm(1, 1 - slot))
    interleave(mm(brp, yb_s, wpb_ref), units(p5a))
    interleave([p5b(q) for q in range(NGB)], [])
    interleave(mm(bgb, xcb_s, wout_ref), [])
    for f in units(p6):
        f()
    if final:
        for b in range(NB):
            unstage(b)()

    @pl.when(i == NT - 1)
    def _():
        _sample_layer(final, xs_ref, mods_ref, h0_ref, cbuf_ref, ws0_ref, bs0_ref, ng_ref,
                      win_ref, vng_ref, cw_ref, cb_ref, wga_ref, wgx_ref, brg_ref, lam_ref,
                      wpa_ref, wpb_ref, wout_ref, fg_ref, xso_ref, hs_ref, cs_ref, vs_ref)


def _sample_layer(final, x_ref, mod_ref, h0_ref, cbuf_ref, ws0_ref, bs0_ref, ng_ref, win_ref,
                  vng_ref, cw_ref, cb_ref, wga_ref, wgx_ref, brg_ref, lam_ref, wpa_ref,
                  wpb_ref, wout_ref, fg_ref, xo_ref, hs_ref, cs_ref, vs_ref):
    x = x_ref[...]
    shift = mod_ref[:, 0:D]
    scale = mod_ref[:, D:2 * D]
    gate = mod_ref[:, 2 * D:3 * D]
    h = ((x * _rms_scale(x) * ng_ref[...]) * (1.0 + scale) + shift).astype(bf16)

    def proj(c0):
        return jnp.dot(h, _as_operand(win_ref[:, c0:c0 + D]), preferred_element_type=f32)

    v = proj(C_V)
    v = v * _rms_scale(v) * vng_ref[...]
    vs_ref[...] = v
    s = ws0_ref[...] * v + bs0_ref[...]
    ya = (proj(C_U) * s * _silu(proj(C_GA))).astype(bf16)

    xb = proj(C_XB)
    xc = (cb_ref[...] + cw_ref[0:1, :] * cbuf_ref[0] + cw_ref[1:2, :] * cbuf_ref[1]
          + cw_ref[2:3, :] * cbuf_ref[2] + cw_ref[3:4, :] * xb)
    cs_ref[0] = cbuf_ref[1]
    cs_ref[1] = cbuf_ref[2]
    cs_ref[2] = xb
    xcb = xc.astype(bf16)
    rp = jnp.concatenate(
        [jnp.dot(xcb[:, q * GBLK:(q + 1) * GBLK], wga_ref[q], preferred_element_type=f32)
         for q in range(NGB)], axis=1)
    ip = jnp.concatenate(
        [jnp.dot(xcb[:, q * GBLK:(q + 1) * GBLK], wgx_ref[q], preferred_element_type=f32)
         for q in range(NGB)], axis=1)
    r = _sigmoid(rp + brg_ref[0:1, :])
    ig = _sigmoid(ip + brg_ref[1:2, :])
    a = jnp.exp2(r * _log2_decay(lam_ref[...]))
    hn = a * h0_ref[...] + _sqrt_unit_interval(1.0 - a * a) * (ig * xc)
    hs_ref[...] = hn
    yb = (hn * _silu(proj(C_GB))).astype(bf16)

    pa = jnp.dot(ya, _as_operand(wpa_ref[...]), preferred_element_type=f32)
    pb = jnp.dot(yb, _as_operand(wpb_ref[...]), preferred_element_type=f32)
    mg = (_sigmoid(proj(C_ZA)) * pa + _sigmoid(proj(C_ZB)) * pb).astype(bf16)
    xn = x + gate * jnp.dot(mg, _as_operand(wout_ref[...]), preferred_element_type=f32)
    if final:
        xn = xn * _rms_scale(xn) * fg_ref[...]
    xo_ref[...] = xn


def _layer_const_spec(shape, l):
    nd = len(shape)
    return pl.BlockSpec((None,) + tuple(shape[1:]), lambda i: (l,) + (0,) * (nd - 1),
                        pipeline_mode=pl.Buffered(1))


def _layer(l, x_in, x_s, mod, ng, win, vng, ws, bs, cw, cb, wga, wgx, brg, lam,
           wpa, wpb, wout, fg, h0, cbuf, ws0, bs0, next_f32):
    final = l == DEPTH - 1
    own = {id(win), id(wpa), id(wpb), id(wout)}
    consts = (ng, win, vng, ws, bs, cw, cb, wga, wgx, brg, lam, wpa, wpb, wout)
    const_specs = [
        pl.BlockSpec(a.shape, lambda i: (0, 0), pipeline_mode=pl.Buffered(1))
        if id(a) in own else _layer_const_spec(a.shape, l) for a in consts]
    sample_stacked = (h0, cbuf, ws0, bs0)
    rows = D // NT
    if final:
        next_f32, cast_in_specs, cast_out_specs, cast_shapes = (), [], [], []
    else:
        cast_in_specs = [pl.BlockSpec((None, rows, w.shape[2]), lambda i: (l + 1, i, 0))
                         for w in next_f32]
        cast_out_specs = [pl.BlockSpec((rows // 2, w.shape[2]), lambda i: (i, 0))
                          for w in next_f32]
        cast_shapes = [jax.ShapeDtypeStruct((w.shape[1] // 2, w.shape[2]), jnp.uint32)
                       for w in next_f32]
    xs_spec = pl.BlockSpec((NS, D), lambda i: (0, 0), pipeline_mode=pl.Buffered(1))
    mods_spec = pl.BlockSpec((None, NS, 3 * D), lambda i: (l, 0, 0),
                             pipeline_mode=pl.Buffered(1))
    mod_spec = pl.BlockSpec((None, NB, 3 * D), lambda i: (l, NS // NB, 0),
                            pipeline_mode=pl.Buffered(1))
    fg_spec = pl.BlockSpec((1, D), lambda i: (0, 0), pipeline_mode=pl.Buffered(1))
    first = l == 0
    if first:
        x_specs = ([pl.BlockSpec((None, TT, D), lambda i, b=b: (b, 0, 0),
                                 pipeline_mode=pl.Buffered(1)) for b in range(NB)]
                   + [pl.BlockSpec((None, TT, D),
                                   lambda i, b=b: (b, jnp.minimum(i + 1, NT - 1), 0))
                      for b in range(NB)])
        extra_scratch = [pltpu.VMEM((2, D // GW, TM, GW), f32)]
    else:
        x_specs = [pl.BlockSpec((TT, NB, D), lambda i: (i, 0, 0)),
                   pl.BlockSpec((TT, NB, D), lambda i: (jnp.minimum(i + 1, NT - 1), 0, 0))]
        extra_scratch = []
    if final:
        y_spec = pl.BlockSpec((NB, TT, D), lambda i: (0, i, 0))
        y_shape = jax.ShapeDtypeStruct((NB, SEQ, D), f32)
        extra_scratch = [pltpu.VMEM((D // GW, TM, GW), f32)]
    else:
        y_spec = pl.BlockSpec((TT, NB, D), lambda i: (i, 0, 0))
        y_shape = jax.ShapeDtypeStruct((SEQ, NB, D), f32)
    return pl.pallas_call(
        functools.partial(_prompt_layer_kernel, final, first),
        grid=(NT,),
        in_specs=x_specs
        + [mod_spec] + const_specs + [fg_spec]
        + [xs_spec, mods_spec] + [_layer_const_spec(a.shape, l) for a in sample_stacked]
        + cast_in_specs,
        out_specs=[
            y_spec,
            pl.BlockSpec((NB, D), lambda i: (0, 0)),
            pl.BlockSpec((TAIL, D), lambda i: (0, 0)),
            pl.BlockSpec((NS, D), lambda i: (0, 0)),
            pl.BlockSpec((NS, D), lambda i: (0, 0)),
            pl.BlockSpec((CONV_W - 1, NS, D), lambda i: (0, 0, 0)),
            pl.BlockSpec((NS, D), lambda i: (0, 0)),
        ] + cast_out_specs,
        out_shape=[
            y_shape,
            jax.ShapeDtypeStruct((NB, D), f32),
            jax.ShapeDtypeStruct((TAIL, D), f32),
            jax.ShapeDtypeStruct((NS, D), f32),
            jax.ShapeDtypeStruct((NS, D), f32),
            jax.ShapeDtypeStruct((CONV_W - 1, NS, D), f32),
            jax.ShapeDtypeStruct((NS, D), f32),
        ] + cast_shapes,
        scratch_shapes=[
            pltpu.VMEM((2, TM, D), bf16),
            pltpu.VMEM((TM, D), f32),
            pltpu.VMEM((TM, D), f32),
            pltpu.VMEM((TM, D), f32),
            pltpu.VMEM((TM, D), f32),
            pltpu.VMEM((TM, D), f32),
            pltpu.VMEM((TM, D), f32),
            pltpu.VMEM((TM, D), f32),
            pltpu.VMEM((TM, D), f32),
            pltpu.VMEM((NG, TM, GW), f32),
            pltpu.VMEM((NG, CHUNK, NB * GW), bf16),
            pltpu.VMEM((TM + TAIL, D), f32),
            pltpu.VMEM((NB, D), f32),
            pltpu.VMEM((TM, D), bf16),
            pltpu.VMEM((TM, D), bf16),
            pltpu.VMEM((TM, D), bf16),
        ] + extra_scratch,
        compiler_params=pltpu.CompilerParams(
            dimension_semantics=("arbitrary",), vmem_limit_bytes=VMEM_LIMIT),
        name="layer_final" if final else ("layer_first" if first else "layer"),
    )(*([x_in] * len(x_specs)), mod, *consts, fg, x_s, mod, *sample_stacked, *next_f32)


def _pack_kernel(*refs):
    n = len(refs) // 2
    for src, dst in zip(refs[:n], refs[n:]):
        dst[...] = pltpu.bitcast(src[0].astype(bf16), jnp.uint32)


def _pack_first_layer(ws):
    rows = 64
    return pl.pallas_call(
        _pack_kernel,
        grid=(D // rows,),
        in_specs=[pl.BlockSpec((1, rows, w.shape[2]), lambda i: (0, i, 0)) for w in ws],
        out_specs=[pl.BlockSpec((rows // 2, w.shape[2]), lambda i: (i, 0)) for w in ws],
        out_shape=[jax.ShapeDtypeStruct((w.shape[1] // 2, w.shape[2]), jnp.uint32)
                   for w in ws],
        compiler_params=pltpu.CompilerParams(dimension_semantics=("arbitrary",)),
        name="pack_first",
    )(*ws)


def _block_diag_gate(w):
    per = GBLK // BW
    w = w.reshape(DEPTH, NGB, per, BW, BW)
    eye = jnp.eye(per, dtype=w.dtype)
    out = jnp.einsum('lqhij,hk->lqhikj', w, eye)
    return out.reshape(DEPTH, NGB, GBLK, GBLK)


def kernel(x_prompt, x_sample, c_prompt, c_sample, state_rglru_h, state_conv, w_ada, b_ada,
           norm_g, w_in, v_norm_g, w_s, b_s, conv_w, conv_b, w_rg_a, b_rg_a, w_rg_x,
           b_rg_x, lam, w_pa, w_pb, w_out, final_g):
    c_all = jnp.concatenate([c_sample, c_prompt, jnp.zeros((8, D), f32)], axis=0)
    mod = _modulation(c_all, w_ada, b_ada)

    big_f32 = (w_in, w_pa, w_pb, w_out)
    win, wpa, wpb, wout = _pack_first_layer(big_f32)
    wga = _block_diag_gate(w_rg_a).astype(bf16)
    wgx = _block_diag_gate(w_rg_x).astype(bf16)
    ws = w_s.astype(bf16)
    bs = jnp.broadcast_to(b_s[..., None], (DEPTH, NG, CHUNK, GW))
    ng = norm_g.reshape(DEPTH, 1, D)
    vng = v_norm_g.reshape(DEPTH, 1, D)
    cb = conv_b.reshape(DEPTH, 1, D)
    brg = jnp.stack([b_rg_a, b_rg_x], axis=1)
    lam3 = lam.reshape(DEPTH, 1, D)
    fg = final_g.reshape(1, D)

    ws0 = jnp.repeat(w_s[:, :, 0, 0], GW, axis=-1).reshape(DEPTH, 1, D)
    bs0 = jnp.repeat(b_s[:, :, 0], GW, axis=-1).reshape(DEPTH, 1, D)
    cbuf = state_conv.transpose(0, 2, 1, 3)

    x_p = x_prompt
    x_s = x_sample.reshape(NS, D)
    hp, cp, hs, cs, vs = [], [], [], [], []
    for l in range(DEPTH):
        x_p, h_l, c_l, x_s, hs_l, cs_l, vs_l, *next_bf16 = _layer(
            l, x_p, x_s, mod, ng, win, vng, ws, bs, conv_w, cb, wga, wgx, brg, lam3,
            wpa, wpb, wout, fg, state_rglru_h, cbuf, ws0, bs0, big_f32)
        if next_bf16:
            win, wpa, wpb, wout = next_bf16
        hp.append(h_l)
        cp.append(c_l.reshape(CONV_W - 1, NB, D).transpose(1, 0, 2))
        hs.append(hs_l)
        cs.append(cs_l.transpose(1, 0, 2))
        vs.append(vs_l)
    y_prompt = x_p
    h_prompt = jnp.stack(hp)
    conv_prompt = jnp.stack(cp)
    y_sample = x_s.reshape(NS, 1, D)
    h_sample = jnp.stack(hs)
    conv_sample = jnp.stack(cs)
    chunk_v_sample = jnp.stack(vs).reshape(DEPTH, NS, 1, D)
    return (y_prompt, y_sample, h_prompt, conv_prompt, h_sample, conv_sample,
            chunk_v_sample)
```
